```python
import math
import jax
import jax.numpy as jnp
from jax import lax
import numpy as np

D_MODEL = 2048
BATCH = 4
SEQ = 4096
DEPTH = 4

GRID_W = 64
CTX_LEN = 256
N_EVEN = (DEPTH + 1) // 2
N_ODD = DEPTH // 2
F32 = jnp.float32
NORM_EPS = 1e-6

A_HEAD_DIM = 64
A_Q_COLS = D_MODEL // 2
A_HEADS = A_Q_COLS // A_HEAD_DIM
A_KV_HEADS = A_HEADS // 8
A_GROUP = A_HEADS // A_KV_HEADS
A_KV_COLS = A_KV_HEADS * A_HEAD_DIM
A_COLS = A_Q_COLS + 2 * A_KV_COLS
A_WINDOW = 128
A_BLOCK = 128
ROPE_BASE = 10000.0
NEG_INF = -1e30

R_HEAD = 64
R_WIDTH = D_MODEL // 2
R_HEADS = R_WIDTH // R_HEAD
R_LORA_W = 64
R_LORA_A = 64
R_LORA_G = 160
R_GN_EPS = 64e-5
R_OFF_W = 3 * R_WIDTH
R_OFF_A = R_OFF_W + 2 * R_LORA_W
R_OFF_G = R_OFF_A + 2 * R_LORA_A
B_COLS = R_OFF_G + R_LORA_G
IN_COLS = A_COLS + B_COLS
MIX_WIDTH = A_Q_COLS + R_WIDTH

HY_WIDTH = D_MODEL
HY_BANDS = 16
HY_EMB = 1 + 2 * HY_BANDS
HY_FILTER_HIDDEN = 64
HY_MIN_DECAY = math.log(1e-2) / 1.5
HY_MAX_DECAY = math.log(1e-2) / 0.3

N_GROUPS = 4
EXPERTS_PER_GROUP = 8
N_EXPERTS = N_GROUPS * EXPERTS_PER_GROUP
TOP_K = 2
EXPERT_HIDDEN = 3 * D_MODEL // 8
EXPERT_BLOCK = 256

kernel_name = 'hybrid_swa_rwkv7_hyena_hmoe_dit'


def rms_norm(x, gain):
    xf = x.astype(F32)
    y = xf * lax.rsqrt(jnp.mean(xf * xf, axis=-1, keepdims=True) + NORM_EPS)
    return (y * gain.astype(F32)).astype(x.dtype)


def ada_norm(x, gain, shift, scale):
    return rms_norm(x, gain) * (1 + scale) + shift


def axial_rope(t, row, col):
    half = A_HEAD_DIM // 2
    inv_freq = ROPE_BASE ** (-jnp.arange(0, half, 2, dtype=F32) / half)

    def rotate(u, pos):
        ang = pos.astype(F32)[:, None] * inv_freq[None, :]
        cos = jnp.cos(ang)[None, :, None, :]
        sin = jnp.sin(ang)[None, :, None, :]
        u1, u2 = u[..., :half // 2], u[..., half // 2:]
        return jnp.concatenate([u1 * cos - u2 * sin, u1 * sin + u2 * cos], axis=-1)

    tf = t.astype(F32)
    return jnp.concatenate([rotate(tf[..., :half], row), rotate(tf[..., half:], col)], axis=-1).astype(t.dtype)


def banded_attention_with_context(q, k, v, kc, vc, sink):
    B, L = q.shape[0], q.shape[1]
    nb = L // A_BLOCK
    nc = kc.shape[1]
    qb = q.reshape(B, nb, A_BLOCK, A_KV_HEADS, A_GROUP, A_HEAD_DIM) * (A_HEAD_DIM ** -0.5)

    def band(t):
        tp = jnp.pad(t, ((0, 0), (A_BLOCK, A_BLOCK), (0, 0), (0, 0)))
        tp = tp.reshape(B, nb + 2, A_BLOCK, A_KV_HEADS, A_HEAD_DIM)
        return jnp.concatenate([tp[:, :-2], tp[:, 1:-1], tp[:, 2:]], axis=2)

    kw, vw = band(k), band(v)
    s_win = jnp.einsum('bnqhgd,bnshd->bhgnqs', qb, kw, preferred_element_type=F32)
    s_ctx = jnp.einsum('bnqhgd,bchd->bhgnqc', qb, kc, preferred_element_type=F32)
    qi = jnp.arange(A_BLOCK)[:, None]
    sj = jnp.arange(3 * A_BLOCK)[None, :]
    key_block = jnp.arange(nb)[:, None, None] - 1 + (sj // A_BLOCK)[None]
    mask = (jnp.abs(sj - A_BLOCK - qi) <= A_WINDOW)[None] & (key_block >= 0) & (key_block < nb)
    s_win = jnp.where(mask, s_win, NEG_INF)
    s_sink = jnp.broadcast_to(sink.astype(F32).reshape(1, A_KV_HEADS, A_GROUP, 1, 1, 1), s_win.shape[:-1] + (1,))
    p = jax.nn.softmax(jnp.concatenate([s_win, s_ctx, s_sink], axis=-1), axis=-1).astype(v.dtype)
    nw = 3 * A_BLOCK
    o = (jnp.einsum('bhgnqs,bnshd->bnqhgd', p[..., :nw], vw)
         + jnp.einsum('bhgnqc,bchd->bnqhgd', p[..., nw:nw + nc], vc))
    return o.reshape(B, L, A_Q_COLS)


def context_attention(q, k, v, sink):
    B, C = q.shape[0], q.shape[1]
    qg = q.reshape(B, C, A_KV_HEADS, A_GROUP, A_HEAD_DIM) * (A_HEAD_DIM ** -0.5)
    s = jnp.einsum('bqhgd,bkhd->bhgqk', qg, k, preferred_element_type=F32)
    s_sink = jnp.broadcast_to(sink.astype(F32).reshape(1, A_KV_HEADS, A_GROUP, 1, 1), s.shape[:-1] + (1,))
    p = jax.nn.softmax(jnp.concatenate([s, s_sink], axis=-1), axis=-1)[..., :C].astype(v.dtype)
    return jnp.einsum('bhgqk,bkhd->bqhgd', p, v).reshape(B, C, A_Q_COLS)


def bidirectional_shift(u, mu):
    prev = jnp.pad(u, ((0, 0), (1, 0), (0, 0)))[:, :-1]
    nxt = jnp.pad(u, ((0, 0), (0, 1), (0, 0)))[:, 1:]
    return u + mu[0] * (prev - u) + mu[1] * (nxt - u)


def orient(t):
    return jnp.stack([t[0], jnp.flip(t[1], axis=1)])


def rwkv_step(state, inp):
    r, w, k, v, a, b = inp
    sa = jnp.einsum('dbhij,dbhj->dbhi', state, a)
    state = state * w[..., None, :] + sa[..., :, None] * b[..., None, :] + v[..., :, None] * k[..., None, :]
    return state, jnp.einsum('dbhij,dbhj->dbhi', state, r)


def rwkv_scan(u, s0, w0, w_up, a0, a_up, k_k, k_a):
    B, L, _ = u.shape
    uf = u.astype(F32)

    def heads(t):
        return t.reshape(t.shape[:-1] + (R_HEADS, R_HEAD))

    r = heads(uf[..., :R_WIDTH])
    k = uf[..., R_WIDTH:2 * R_WIDTH]
    v = heads(uf[..., 2 * R_WIDTH:3 * R_WIDTH])
    wd = uf[..., R_OFF_W:R_OFF_A].reshape(B, L, 2, R_LORA_W)
    ad = uf[..., R_OFF_A:R_OFF_G].reshape(B, L, 2, R_LORA_A)
    w_log = w0.astype(F32)[:, None, None, :] + jnp.einsum('bldr,drc->dblc', jnp.tanh(wd), w_up.astype(F32))
    decay = jnp.exp(-jnp.exp(-jax.nn.softplus(-w_log) - 0.5))
    a = jax.nn.sigmoid(a0.astype(F32)[:, None, None, :] + jnp.einsum('bldr,drc->dblc', ad, a_up.astype(F32)))
    kk = heads(k * k_k.astype(F32))
    kk = kk * lax.rsqrt(jnp.maximum(jnp.sum(kk * kk, axis=-1, keepdims=True), 1e-24))
    k_dir = heads(k[None] * (1 + (a - 1) * k_a.astype(F32)))
    xs = (r[None], heads(decay), k_dir, v[None], -kk[None], kk[None] * heads(a))
    xs = tuple(jnp.moveaxis(orient(jnp.broadcast_to(t, (2,) + t.shape[1:])), 2, 0) for t in xs)
    state, ys = lax.scan(rwkv_step, s0, xs)
    y = orient(jnp.moveaxis(ys, 0, 2))
    return (y, r, k_dir, v), state


def rwkv_readout(u, terms, g_up, r_k, lnx_w, lnx_b):
    y, r, k_dir, v = terms
    B, L, _ = u.shape
    ys = y[0] + y[1]
    mean = jnp.mean(ys, axis=-1, keepdims=True)
    var = jnp.mean(jnp.square(ys - mean), axis=-1, keepdims=True)
    yn = ((ys - mean) * lax.rsqrt(var + R_GN_EPS)).reshape(B, L, R_WIDTH) * lnx_w.astype(F32) + lnx_b.astype(F32)
    bonus = jnp.sum(jnp.sum(r[None] * k_dir * r_k.astype(F32), axis=-1, keepdims=True) * v[None], axis=0)
    g = jax.nn.sigmoid(u[..., R_OFF_G:].astype(F32)) @ g_up.astype(F32)
    return (yn + bonus.reshape(B, L, R_WIDTH)) * g


def attention_rwkv_layer(hl, hc, row_ids, col_ids, ctx_out, w_in, w_out, sink, shift_mu,
                         w0, w_up, a0, a_up, g_up, k_k, k_a, r_k, lnx_w, lnx_b):
    B, L, _ = hl.shape
    C = hc.shape[1]
    zl = hl @ w_in
    zc = hc @ w_in

    def kv(z, n):
        k = z[..., A_Q_COLS:A_Q_COLS + A_KV_COLS].reshape(B, n, A_KV_HEADS, A_HEAD_DIM)
        v = z[..., A_Q_COLS + A_KV_COLS:A_COLS].reshape(B, n, A_KV_HEADS, A_HEAD_DIM)
        return k, v

    ql = axial_rope(zl[..., :A_Q_COLS].reshape(B, L, A_HEADS, A_HEAD_DIM), row_ids, col_ids)
    kl, vl = kv(zl, L)
    kl = axial_rope(kl, row_ids, col_ids)
    kc, vc = kv(zc, C)
    att_l = banded_attention_with_context(ql, kl, vl, kc, vc, sink)

    ul = bidirectional_shift(zl[..., A_COLS:], shift_mu)
    uc = bidirectional_shift(zc[..., A_COLS:], shift_mu)
    s0 = jnp.zeros((2, B, R_HEADS, R_HEAD, R_HEAD), F32)
    ctx_terms, ctx_state = rwkv_scan(uc, s0, w0, w_up, a0, a_up, k_k, k_a)
    lat_terms, _ = rwkv_scan(ul, ctx_state, w0, w_up, a0, a_up, k_k, k_a)
    rw_l = rwkv_readout(ul, lat_terms, g_up, r_k, lnx_w, lnx_b)
    out_l = jnp.concatenate([att_l, rw_l.astype(att_l.dtype)], axis=-1) @ w_out
    if not ctx_out:
        return out_l, None
    qc = zc[..., :A_Q_COLS].reshape(B, C, A_HEADS, A_HEAD_DIM)
    att_c = context_attention(qc, kc, vc, sink)
    rw_c = rwkv_readout(uc, ctx_terms, g_up, r_k, lnx_w, lnx_b)
    out_c = jnp.concatenate([att_c, rw_c.astype(att_c.dtype)], axis=-1) @ w_out
    return out_l, out_c


def centred_conv3(u, w, b):
    up = jnp.pad(u, ((0, 0), (1, 1), (0, 0)))
    return up[:, :-2] * w[0] + up[:, 1:-1] * w[1] + up[:, 2:] * w[2] + b


def hyena_filters(L, w1, b1, f1, w2, b2, f2, w3):
    t = jnp.linspace(0.0, 1.0, L, dtype=F32)[:, None]
    bands = jnp.linspace(1e-4, HY_BANDS - 1, HY_BANDS, dtype=F32)[None, :]
    ang = (2 * math.pi) * jnp.arange(L, dtype=F32)[:, None] / L * bands
    z = jnp.concatenate([t, jnp.cos(ang), -jnp.sin(ang)], axis=-1)
    h = jnp.sin(f1.astype(F32) * (z @ w1.astype(F32) + b1.astype(F32)))
    h = jnp.sin(f2.astype(F32) * (h @ w2.astype(F32) + b2.astype(F32)))
    h = h @ w3.astype(F32)
    deltas = jnp.abs(jnp.linspace(HY_MIN_DECAY, HY_MAX_DECAY, HY_WIDTH, dtype=F32))
    h = h * jnp.exp(-t * jnp.tile(deltas, 2)[None, :])
    h_fwd, h_bwd = h[:, :HY_WIDTH], h[:, HY_WIDTH:]
    k_full = jnp.concatenate([h_fwd, jnp.zeros((1, HY_WIDTH), F32), jnp.flip(h_bwd[1:], axis=0)], axis=0)
    return k_full / jnp.sum(jnp.abs(k_full), axis=0, keepdims=True)


def bidirectional_long_conv(u, k_full):
    L = u.shape[1]
    n = 2 * L
    u_f = jnp.fft.rfft(u, n=n, axis=1)
    k_f = jnp.fft.rfft(k_full, n=n, axis=0)
    return jnp.fft.irfft(u_f * k_f[None], n=n, axis=1)[:, :L]


def hyena_mixer(h, w_in, b_in, conv_w, conv_b, fw1, fb1, ff1, fw2, fb2, ff2, fw3, bias, w_out):
    L = h.shape[1]
    z = centred_conv3(h @ w_in + b_in, conv_w, conv_b)
    x0, x1, v = jnp.split(z, 3, axis=-1)
    u = (v * x1).astype(F32)
    y = bidirectional_long_conv(u, hyena_filters(L, fw1, fb1, ff1, fw2, fb2, ff2, fw3)) + u * bias.astype(F32)
    return (y.astype(h.dtype) * x0) @ w_out


def grouped_experts(x, expert, gate, w_gate, w_up, w_down):
    T, D = x.shape
    A = T * TOP_K
    e_flat = expert.reshape(A)
    order = jnp.argsort(e_flat)
    e_sorted = e_flat[order]
    tok = order // TOP_K
    counts = jnp.bincount(e_flat, length=N_EXPERTS)
    padded = (counts + EXPERT_BLOCK - 1) // EXPERT_BLOCK * EXPERT_BLOCK
    pad_end = jnp.cumsum(padded)
    pad_start = pad_end - padded
    seg_start = jnp.cumsum(counts) - counts
    dest = pad_start[e_sorted] + jnp.arange(A) - seg_start[e_sorted]
    n_blocks = -(-A // EXPERT_BLOCK) + N_EXPERTS
    buf = jnp.zeros((n_blocks * EXPERT_BLOCK, D), x.dtype).at[dest].set(x[tok])
    block_expert = jnp.minimum(
        jnp.searchsorted(pad_end, jnp.arange(n_blocks) * EXPERT_BLOCK, side='right'), N_EXPERTS - 1)

    def expert_block(args):
        xb, e = args
        hid = jax.nn.silu(xb @ w_gate[e]) * (xb @ w_up[e])
        return hid @ w_down[e]

    yb = lax.map(expert_block, (buf.reshape(n_blocks, EXPERT_BLOCK, D), block_expert))
    y = yb.reshape(n_blocks * EXPERT_BLOCK, D)[dest]
    w_sorted = gate.reshape(A)[order].astype(y.dtype)
    return jax.ops.segment_sum(y * w_sorted[:, None], tok, num_segments=T)


def hierarchical_moe(x, w_route_group, w_route_expert, w_gate, w_up, w_down):
    T = x.shape[0]
    xf = x.astype(F32)
    lg = xf @ w_route_group.astype(F32)
    g_idx = jnp.argmax(lg, axis=-1)
    p_group = jnp.take_along_axis(jax.nn.softmax(lg, axis=-1), g_idx[:, None], axis=-1)
    le = (xf @ w_route_expert.astype(F32)).reshape(T, N_GROUPS, EXPERTS_PER_GROUP)
    le_sel = jnp.take_along_axis(le, g_idx[:, None, None], axis=1)[:, 0]
    top_v, top_i = lax.top_k(le_sel, TOP_K)
    gate = p_group * jax.nn.softmax(top_v, axis=-1)
    expert = g_idx[:, None] * EXPERTS_PER_GROUP + top_i
    return grouped_experts(x, expert, gate, w_gate, w_up, w_down)


def setup_inputs(seed: int = 0) -> dict:
    key = jax.random.key(seed)
    ks = jax.random.split(key, 48)

    def nrm(i, shape, scale):
        return jax.random.normal(ks[i], shape, F32) * scale

    D = D_MODEL
    E, O = N_EVEN, N_ODD
    ramp = jnp.linspace(-6.5, -1.5, R_WIDTH, dtype=F32)
    return {
        'x': nrm(0, (BATCH, SEQ, D), 1.0),
        'c': nrm(1, (BATCH, D), 1.0),
        'ctx': nrm(2, (BATCH, CTX_LEN, D), 1.0),
        'c_ctx': nrm(3, (D,), 1.0),
        'mod_w': nrm(4, (DEPTH, D, 6 * D), 0.5 * D ** -0.5),
        'mod_b': nrm(5, (DEPTH, 6 * D), 0.02),
        'norm1_w': 1.0 + nrm(6, (DEPTH, D), 0.02),
        'norm2_w': 1.0 + nrm(7, (DEPTH, D), 0.02),
        'router_group_w': nrm(8, (DEPTH, D, N_GROUPS), D ** -0.5),
        'router_expert_w': nrm(9, (DEPTH, D, N_EXPERTS), D ** -0.5),
        'expert_w_gate': nrm(10, (DEPTH, N_EXPERTS, D, EXPERT_HIDDEN), D ** -0.5),
        'expert_w_up': nrm(11, (DEPTH, N_EXPERTS, D, EXPERT_HIDDEN), D ** -0.5),
        'expert_w_down': nrm(12, (DEPTH, N_EXPERTS, EXPERT_HIDDEN, D), EXPERT_HIDDEN ** -0.5),
        'mix_w_in': nrm(13, (E, D, IN_COLS), D ** -0.5),
        'mix_w_out': nrm(14, (E, MIX_WIDTH, D), MIX_WIDTH ** -0.5),
        'attn_sink': nrm(15, (E, A_HEADS), 0.5),
        'rwkv_shift_mu': jax.random.uniform(ks[16], (E, 2, B_COLS), F32, 0.0, 0.5),
        'rwkv_w0': ramp + nrm(17, (E, 2, R_WIDTH), 0.1),
        'rwkv_w_up': nrm(18, (E, 2, R_LORA_W, R_WIDTH), 0.1),
        'rwkv_a0': nrm(19, (E, 2, R_WIDTH), 0.1),
        'rwkv_a_up': nrm(20, (E, 2, R_LORA_A, R_WIDTH), 0.1),
        'rwkv_g_up': nrm(21, (E, R_LORA_G, R_WIDTH), R_LORA_G ** -0.5),
        'rwkv_k_k': 0.85 + nrm(22, (E, R_WIDTH), 0.02),
        'rwkv_k_a': 1.0 + nrm(23, (E, R_WIDTH), 0.02),
        'rwkv_r_k': nrm(24, (E, R_HEADS, R_HEAD), 0.1),
        'rwkv_lnx_w': 1.0 + nrm(25, (E, R_WIDTH), 0.02),
        'rwkv_lnx_b': nrm(26, (E, R_WIDTH), 0.02),
        'hy_w_in': nrm(27, (O, D, 3 * HY_WIDTH), D ** -0.5),
        'hy_b_in': nrm(28, (O, 3 * HY_WIDTH), 0.02),
        'hy_conv_w': nrm(29, (O, 3, 3 * HY_WIDTH), 3 ** -0.5),
        'hy_conv_b': nrm(30, (O, 3 * HY_WIDTH), 0.02),
        'hy_ffn_w1': nrm(31, (O, HY_EMB, HY_FILTER_HIDDEN), HY_EMB ** -0.5),
        'hy_ffn_b1': nrm(32, (O, HY_FILTER_HIDDEN), 0.1),
        'hy_sin_f1': 1.0 + nrm(33, (O, HY_FILTER_HIDDEN), 0.02),
        'hy_ffn_w2': nrm(34, (O, HY_FILTER_HIDDEN, HY_FILTER_HIDDEN), HY_FILTER_HIDDEN ** -0.5),
        'hy_ffn_b2': nrm(35, (O, HY_FILTER_HIDDEN), 0.1),
        'hy_sin_f2': 1.0 + nrm(36, (O, HY_FILTER_HIDDEN), 0.02),
        'hy_ffn_w3': nrm(37, (O, HY_FILTER_HIDDEN, 2 * HY_WIDTH), HY_FILTER_HIDDEN ** -0.5),
        'hy_bias': nrm(38, (O, HY_WIDTH), 1.0),
        'hy_w_out': nrm(39, (O, HY_WIDTH, D), HY_WIDTH ** -0.5),
        'final_norm_w': 1.0 + nrm(40, (D,), 0.02),
    }


def reference(x, c, ctx, c_ctx, mod_w, mod_b, norm1_w, norm2_w, router_group_w, router_expert_w,
              expert_w_gate, expert_w_up, expert_w_down, mix_w_in, mix_w_out, attn_sink, rwkv_shift_mu,
              rwkv_w0, rwkv_w_up, rwkv_a0, rwkv_a_up, rwkv_g_up, rwkv_k_k, rwkv_k_a, rwkv_r_k,
              rwkv_lnx_w, rwkv_lnx_b, hy_w_in, hy_b_in, hy_conv_w, hy_conv_b, hy_ffn_w1, hy_ffn_b1,
              hy_sin_f1, hy_ffn_w2, hy_ffn_b2, hy_sin_f2, hy_ffn_w3, hy_bias, hy_w_out, final_norm_w):
    B, L, D = x.shape
    rows = L // GRID_W
    row_ids = jnp.repeat(jnp.arange(rows), GRID_W)
    col_ids = jnp.tile(jnp.arange(GRID_W), rows)
    xl, xc = x, ctx
    for i in range(DEPTH):
        ctx_later = any(j % 2 == 0 for j in range(i + 1, DEPTH))
        ctx_here = ctx_later or (i % 2 == 0)
        mod_l = (jax.nn.silu(c) @ mod_w[i] + mod_b[i]).reshape(B, 1, 6, D)
        hl = ada_norm(xl, norm1_w[i], mod_l[:, :, 0], mod_l[:, :, 1])
        if ctx_here:
            mod_c = (jax.nn.silu(c_ctx) @ mod_w[i] + mod_b[i]).reshape(1, 1, 6, D)
            hc = ada_norm(xc, norm1_w[i], mod_c[:, :, 0], mod_c[:, :, 1])
        if i % 2 == 0:
            e = i // 2
            ml, mc = attention_rwkv_layer(
                hl, hc, row_ids, col_ids, ctx_later, mix_w_in[e], mix_w_out[e], attn_sink[e],
                rwkv_shift_mu[e], rwkv_w0[e], rwkv_w_up[e], rwkv_a0[e], rwkv_a_up[e], rwkv_g_up[e],
                rwkv_k_k[e], rwkv_k_a[e], rwkv_r_k[e], rwkv_lnx_w[e], rwkv_lnx_b[e])
        else:
            o = i // 2
            hy = (hy_w_in[o], hy_b_in[o], hy_conv_w[o], hy_conv_b[o], hy_ffn_w1[o], hy_ffn_b1[o],
                  hy_sin_f1[o], hy_ffn_w2[o], hy_ffn_b2[o], hy_sin_f2[o], hy_ffn_w3[o], hy_bias[o], hy_w_out[o])
            ml = hyena_mixer(hl, *hy)
            mc = hyena_mixer(hc, *hy) if ctx_later else None
        xl = xl + mod_l[:, :, 2] * ml
        hl2 = ada_norm(xl, norm2_w[i], mod_l[:, :, 3], mod_l[:, :, 4])
        moe = (router_group_w[i], router_expert_w[i], expert_w_gate[i], expert_w_up[i], expert_w_down[i])
        if ctx_later:
            xc = xc + mod_c[:, :, 2] * mc
            hc2 = ada_norm(xc, norm2_w[i], mod_c[:, :, 3], mod_c[:, :, 4])
            f = hierarchical_moe(jnp.concatenate([hl2.reshape(-1, D), hc2.reshape(-1, D)], axis=0), *moe)
            xl = xl + mod_l[:, :, 5] * f[:B * L].reshape(B, L, D)
            xc = xc + mod_c[:, :, 5] * f[B * L:].reshape(xc.shape)
        else:
            xl = xl + mod_l[:, :, 5] * hierarchical_moe(hl2.reshape(-1, D), *moe).reshape(B, L, D)
    return rms_norm(xl, final_norm_w)
```

```python
import functools
import math

import jax
import jax.numpy as jnp
from jax import lax
from jax.experimental import pallas as pl
from jax.experimental.pallas import tpu as pltpu

F32 = jnp.float32
BF16 = jnp.bfloat16

GRID_W = 64
NORM_EPS = 1e-6
A_HEAD_DIM = 64
A_GROUP = 8
A_WINDOW = 128
A_BLOCK = 128
ROPE_BASE = 10000.0
NEG_INF = -1e30
R_HEAD = 64
R_LORA_W = 64
R_LORA_A = 64
R_GN_EPS = 64e-5
HY_BANDS = 16
HY_MIN_DECAY = math.log(1e-2) / 1.5
HY_MAX_DECAY = math.log(1e-2) / 0.3
N_GROUPS = 4
EXPERTS_PER_GROUP = 8
N_EXPERTS = N_GROUPS * EXPERTS_PER_GROUP
TOP_K = 2

LANE = 128
SUBLANE = 8
VMEM_LIMIT = 56 * 1024 * 1024
MM_TILE_BYTES = 8 * 1024 * 1024
RWKV_CHUNK = 64
MOE_BLOCK = 256


def _pick(n, cands):
    for c in cands:
        if n % c == 0:
            return c
    return n


def _split_bf16(x):
    hi = x.astype(BF16)
    lo = (x - hi.astype(F32)).astype(BF16)
    return hi, lo


def _dg(a, b, dims):
    return lax.dot_general(a, b, dims, preferred_element_type=F32)


def _dot3(a, b, dims):
    ah, al = _split_bf16(a)
    bh, bl = _split_bf16(b)
    return _dg(ah, bh, dims) + (_dg(ah, bl, dims) + _dg(al, bh, dims))


def _mm_body(*refs, norm, bias, res, x3, emit_h, w3d):
    it = iter(refs)
    x_ref = next(it)
    w_ref = next(it)
    g_ref = next(it) if norm else None
    s_ref = next(it) if norm else None
    b_ref = next(it) if bias else None
    r_ref = next(it) if res else None
    gt_ref = next(it) if res else None
    o_ref = next(it)
    h_ref = next(it) if emit_h else None
    xh_ref = next(it)
    xl_ref = next(it) if x3 else None

    @pl.when(pl.program_id(1) == 0)
    def _():
        x = x_ref[...].astype(F32)
        if norm:
            ms = jnp.mean(x * x, axis=-1, keepdims=True)
            x = x * lax.rsqrt(ms + NORM_EPS) * g_ref[0] + s_ref[0]
        xh = x.astype(BF16)
        xh_ref[...] = xh
        if x3:
            xl_ref[...] = (x - xh.astype(F32)).astype(BF16)
        if emit_h:
            h_ref[...] = x.astype(h_ref.dtype)

    w = w_ref[0] if w3d else w_ref[...]
    wh = w.astype(BF16)
    acc = jnp.dot(xh_ref[...], wh, preferred_element_type=F32)
    if x3:
        wl = (w.astype(F32) - wh.astype(F32)).astype(BF16)
        acc = acc + (jnp.dot(xh_ref[...], wl, preferred_element_type=F32)
                     + jnp.dot(xl_ref[...], wh, preferred_element_type=F32))
    if bias:
        acc = acc + b_ref[...]
    if res:
        acc = r_ref[...] + gt_ref[0] * acc
    o_ref[...] = acc.astype(o_ref.dtype)


def _mm_tiles(M, K, N, rows_per_group, w_itemsize):
    tm = _pick(math.gcd(M, rows_per_group), (512, 256, 128, 64, 32, 16, 8))
    cands = [t for t in (2048, 1536, 1024, 768, 640, 512, 384, 256, 128)
             if K * t * w_itemsize <= MM_TILE_BYTES and tm * t * 4 <= MM_TILE_BYTES]
    tn = N if (N <= cands[0] and N % LANE == 0) else _pick(N, cands)
    return tm, tn


def mm(x, w, *, w_idx=None, norm=None, bias=None, res=None, rows_per_group=None, x3=False,
       emit_h=None, out_dtype=F32, tn=None):
    M, K = x.shape
    N = w.shape[-1]
    if rows_per_group is None:
        rows_per_group = M
    tm, tn_auto = _mm_tiles(M, K, N, rows_per_group, w.dtype.itemsize)
    tn = tn_auto if tn is None else tn
    assert M % tm == 0 and N % tn == 0 and rows_per_group % tm == 0
    bpg = rows_per_group // tm

    if w_idx is None:
        w_spec = pl.BlockSpec((K, tn), lambda i, j: (0, j))
    else:
        w_spec = pl.BlockSpec((1, K, tn), lambda i, j: (w_idx, 0, j))
    in_specs = [pl.BlockSpec((tm, K), lambda i, j: (i, 0)), w_spec]
    args = [x, w]
    if norm is not None:
        in_specs += [pl.BlockSpec((1, 1, K), lambda i, j: (i // bpg, 0, 0))] * 2
        args += [norm[0], norm[1]]
    if bias is not None:
        in_specs.append(pl.BlockSpec((1, tn), lambda i, j: (0, j)))
        args.append(bias.reshape(1, N).astype(F32))
    if res is not None:
        in_specs.append(pl.BlockSpec((tm, tn), lambda i, j: (i, j)))
        in_specs.append(pl.BlockSpec((1, 1, tn), lambda i, j: (i // bpg, 0, j)))
        args += [res[0], res[1]]
    out_shape = [jax.ShapeDtypeStruct((M, N), out_dtype)]
    out_specs = [pl.BlockSpec((tm, tn), lambda i, j: (i, j))]
    if emit_h is not None:
        out_shape.append(jax.ShapeDtypeStruct((M, K), emit_h))
        out_specs.append(pl.BlockSpec((tm, K), lambda i, j: (i, 0)))
    scratch = [pltpu.VMEM((tm, K), BF16)]
    if x3:
        scratch.append(pltpu.VMEM((tm, K), BF16))
    body = functools.partial(_mm_body, norm=norm is not None, bias=bias is not None,
                             res=res is not None, x3=x3, emit_h=emit_h is not None,
                             w3d=w_idx is not None)
    outs = pl.pallas_call(
        body,
        grid=(M // tm, N // tn),
        in_specs=in_specs,
        out_specs=out_specs,
        out_shape=out_shape,
        scratch_shapes=scratch,
        compiler_params=pltpu.CompilerParams(
            dimension_semantics=("parallel", "arbitrary"), vmem_limit_bytes=VMEM_LIMIT),
    )(*args)
    return outs if emit_h is not None else outs[0]


def _attn_body(sink_ref, q_ref, k_ref, v_ref, kc_ref, vc_ref, o_ref, *, seq, n_kv, band):
    n = pl.program_id(1)
    win = 3 * A_BLOCK if band else seq
    if band:
        start = jnp.clip((n - 1) * A_BLOCK, 0, seq - win)
        start = pl.multiple_of(start, A_BLOCK)
        qpos = n * A_BLOCK + lax.broadcasted_iota(jnp.int32, (A_BLOCK, win), 0)
        kpos = start + lax.broadcasted_iota(jnp.int32, (A_BLOCK, win), 1)
        mask = jnp.abs(kpos - qpos) <= A_WINDOW
    for h in range(n_kv):
        if band:
            kw = k_ref[0, h, pl.ds(start, win), :]
            vw = v_ref[0, h, pl.ds(start, win), :]
        kc = kc_ref[0, h]
        vc = vc_ref[0, h]
        for g in range(A_GROUP):
            hd = h * A_GROUP + g
            q = q_ref[0, hd]
            sink = sink_ref[hd]
            s_c = _dg(q, kc, (((1,), (1,)), ((), ())))
            m = jnp.maximum(jnp.max(s_c, axis=-1, keepdims=True), sink)
            if band:
                s_w = _dg(q, kw, (((1,), (1,)), ((), ())))
                s_w = jnp.where(mask, s_w, NEG_INF)
                m = jnp.maximum(m, jnp.max(s_w, axis=-1, keepdims=True))
                p_w = jnp.exp(s_w - m)
            p_c = jnp.exp(s_c - m)
            den = jnp.sum(p_c, axis=-1, keepdims=True) + jnp.exp(sink - m)
            o = jnp.dot(p_c.astype(BF16), vc, preferred_element_type=F32)
            if band:
                den = den + jnp.sum(p_w, axis=-1, keepdims=True)
                o = o + jnp.dot(p_w.astype(BF16), vw, preferred_element_type=F32)
            o_ref[0, hd] = o / den


def attention(q, k, v, kc, vc, sink, *, band):
    B, H, L, dh = q.shape
    n_kv = kc.shape[1]
    C = kc.shape[2]
    nb = L // A_BLOCK
    body = functools.partial(_attn_body, seq=L, n_kv=n_kv, band=band)
    kv_spec = pl.BlockSpec((1, n_kv, k.shape[2], dh), lambda b, n: (b, 0, 0, 0))
    return pl.pallas_call(
        body,
        grid=(B, nb),
        in_specs=[pl.BlockSpec(memory_space=pltpu.SMEM),
                  pl.BlockSpec((1, H, A_BLOCK, dh), lambda b, n: (b, 0, n, 0)),
                  kv_spec, kv_spec,
                  pl.BlockSpec((1, n_kv, C, dh), lambda b, n: (b, 0, 0, 0)),
                  pl.BlockSpec((1, n_kv, C, dh), lambda b, n: (b, 0, 0, 0))],
        out_specs=pl.BlockSpec((1, H, A_BLOCK, dh), lambda b, n: (b, 0, n, 0)),
        out_shape=jax.ShapeDtypeStruct((B, H, L, dh), F32),
        compiler_params=pltpu.CompilerParams(
            dimension_semantics=("parallel", "arbitrary"), vmem_limit_bytes=VMEM_LIMIT),
    )(sink.astype(F32), q, k, v, kc, vc)


def _rwkv_body(at_ref, bt_ref, kt_ref, rt_ref, v_ref, gc_ref, y_ref, st_ref):
    C = RWKV_CHUNK

    @pl.when(pl.program_id(1) == 0)
    def _():
        st_ref[...] = jnp.zeros_like(st_ref)

    at = at_ref[0]
    bt = bt_ref[0]
    kt = kt_ref[0]
    rt = rt_ref[0]
    v = v_ref[0]
    gc = gc_ref[0, 0]
    m0 = st_ref[...]
    nh = at.shape[0]
    bmm = (((2,), (1,)), ((0,), (0,)))
    bmt = (((2,), (2,)), ((0,), (0,)))
    btm = (((1,), (1,)), ((0,), (0,)))

    row = lax.broadcasted_iota(jnp.int32, (C, C), 0)
    col = lax.broadcasted_iota(jnp.int32, (C, C), 1)
    strict = (row > col)[None]
    incl = (row >= col)[None]
    eye = (row == col).astype(F32)[None]

    ar = jnp.concatenate([at, rt], axis=1)
    bk = jnp.concatenate([bt, kt], axis=1)
    big = _dot3(ar, bk, bmt)
    a_ab = jnp.where(strict, big[:, :C, :C], 0.0)
    a_ak = jnp.where(strict, big[:, :C, C:], 0.0)
    a_rb = jnp.where(incl, big[:, C:, :C], 0.0)
    a_rk = jnp.where(incl, big[:, C:, C:], 0.0)

    tinv = eye + a_ab
    p = a_ab
    for _ in range(int(math.log2(C)) - 1):
        p = _dot3(p, p, bmm)
        tinv = tinv + _dot3(tinv, p, bmm)

    akv = _dot3(a_ak, v, bmm)
    w12 = _dot3(tinv, jnp.concatenate([at, akv], axis=2), bmm)
    rb12 = _dot3(a_rb, w12, bmm)
    rq = rt + rb12[:, :, :at.shape[2]]
    yc = rb12[:, :, at.shape[2]:] + _dot3(a_rk, v, bmm)
    w1 = w12[:, :, :at.shape[2]]
    w2 = w12[:, :, at.shape[2]:]

    um = _dot3(jnp.concatenate([w1, rq], axis=1), m0, bmt)
    u = um[:, :C] + w2
    y_ref[0] = um[:, C:] + yc
    upd = _dot3(jnp.concatenate([u, v], axis=1), bk, btm)
    st_ref[...] = (m0 + upd) * gc


def rwkv_chunk_scan(at, bt, kt, rt, v, gc):
    S, H, T, N = at.shape
    C = RWKV_CHUNK
    nc = T // C
    spec = pl.BlockSpec((1, H, C, N), lambda s, c: (s, 0, c, 0))
    return pl.pallas_call(
        _rwkv_body,
        grid=(S, nc),
        in_specs=[spec, spec, spec, spec, spec,
                  pl.BlockSpec((1, 1, H, 1, N), lambda s, c: (s, c, 0, 0, 0))],
        out_specs=spec,
        out_shape=jax.ShapeDtypeStruct((S, H, T, N), F32),
        scratch_shapes=[pltpu.VMEM((H, N, N), F32)],
        compiler_params=pltpu.CompilerParams(
            dimension_semantics=("parallel", "arbitrary"), vmem_limit_bytes=VMEM_LIMIT),
    )(at, bt, kt, rt, v, gc)


def _moe_up_body(be_ref, nu_ref, x_ref, wg_ref, wu_ref, o_ref, wgs_ref, wus_ref):
    i = pl.program_id(0)
    prev = be_ref[jnp.maximum(i - 1, 0)]
    fresh = jnp.logical_or(i == 0, be_ref[i] != prev)

    @pl.when(fresh)
    def _():
        wgs_ref[...] = wg_ref[0, 0].astype(BF16)
        wus_ref[...] = wu_ref[0, 0].astype(BF16)

    @pl.when(i < nu_ref[0])
    def _():
        x = x_ref[...]
        g = jnp.dot(x, wgs_ref[...], preferred_element_type=F32)
        u = jnp.dot(x, wus_ref[...], preferred_element_type=F32)
        o_ref[...] = (g * jax.nn.sigmoid(g) * u).astype(o_ref.dtype)

    @pl.when(i >= nu_ref[0])
    def _():
        o_ref[...] = jnp.zeros_like(o_ref)


def _moe_down_body(be_ref, nu_ref, h_ref, wd_ref, o_ref, wds_ref):
    i = pl.program_id(0)
    prev = be_ref[jnp.maximum(i - 1, 0)]
    fresh = jnp.logical_or(i == 0, be_ref[i] != prev)

    @pl.when(fresh)
    def _():
        wds_ref[...] = wd_ref[0, 0].astype(BF16)

    @pl.when(i < nu_ref[0])
    def _():
        o_ref[...] = jnp.dot(h_ref[...], wds_ref[...], preferred_element_type=F32).astype(o_ref.dtype)

    @pl.when(i >= nu_ref[0])
    def _():
        o_ref[...] = jnp.zeros_like(o_ref)


def moe_experts(xs, block_expert, n_used, w_gate, w_up, w_down, layer):
    R, D = xs.shape
    Hd = w_gate.shape[-1]
    nblk = R // MOE_BLOCK
    hid = pl.pallas_call(
        _moe_up_body,
        grid_spec=pltpu.PrefetchScalarGridSpec(
            num_scalar_prefetch=2,
            grid=(nblk,),
            in_specs=[pl.BlockSpec((MOE_BLOCK, D), lambda i, be, nu: (i, 0)),
                      pl.BlockSpec((1, 1, D, Hd), lambda i, be, nu: (layer, be[i], 0, 0)),
                      pl.BlockSpec((1, 1, D, Hd), lambda i, be, nu: (layer, be[i], 0, 0))],
            out_specs=pl.BlockSpec((MOE_BLOCK, Hd), lambda i, be, nu: (i, 0)),
            scratch_shapes=[pltpu.VMEM((D, Hd), BF16), pltpu.VMEM((D, Hd), BF16)]),
        out_shape=jax.ShapeDtypeStruct((R, Hd), BF16),
        compiler_params=pltpu.CompilerParams(
            dimension_semantics=("arbitrary",), vmem_limit_bytes=VMEM_LIMIT),
    )(block_expert, n_used, xs, w_gate, w_up)
    return pl.pallas_call(
        _moe_down_body,
        grid_spec=pltpu.PrefetchScalarGridSpec(
            num_scalar_prefetch=2,
            grid=(nblk,),
            in_specs=[pl.BlockSpec((MOE_BLOCK, Hd), lambda i, be, nu: (i, 0)),
                      pl.BlockSpec((1, 1, Hd, D), lambda i, be, nu: (layer, be[i], 0, 0))],
            out_specs=pl.BlockSpec((MOE_BLOCK, D), lambda i, be, nu: (i, 0)),
            scratch_shapes=[pltpu.VMEM((Hd, D), BF16)]),
        out_shape=jax.ShapeDtypeStruct((R, D), F32),
        compiler_params=pltpu.CompilerParams(
            dimension_semantics=("arbitrary",), vmem_limit_bytes=VMEM_LIMIT),
    )(block_expert, n_used, hid, w_down)


def moe_layer(h_bf16, logits, w_gate, w_up, w_down, layer):
    T, D = h_bf16.shape
    lg = logits[:, :N_GROUPS]
    g_idx = jnp.argmax(lg, axis=-1)
    p_group = jnp.take_along_axis(jax.nn.softmax(lg, axis=-1), g_idx[:, None], axis=-1)
    le = logits[:, N_GROUPS:N_GROUPS + N_EXPERTS].reshape(T, N_GROUPS, EXPERTS_PER_GROUP)
    le_sel = jnp.take_along_axis(le, g_idx[:, None, None], axis=1)[:, 0]
    top_v, top_i = lax.top_k(le_sel, TOP_K)
    gate = p_group * jax.nn.softmax(top_v, axis=-1)
    expert = (g_idx[:, None] * EXPERTS_PER_GROUP + top_i).astype(jnp.int32)

    A = T * TOP_K
    e_flat = expert.reshape(A)
    onehot = (e_flat[:, None] == jnp.arange(N_EXPERTS, dtype=jnp.int32)[None, :]).astype(jnp.int32)
    rank = jnp.take_along_axis(jnp.cumsum(onehot, axis=0), e_flat[:, None], axis=1)[:, 0] - 1
    counts = jnp.sum(onehot, axis=0)
    padded = (counts + MOE_BLOCK - 1) // MOE_BLOCK * MOE_BLOCK
    pad_end = jnp.cumsum(padded)
    pad_start = pad_end - padded
    dest = pad_start[e_flat] + rank
    nblk = -(-A // MOE_BLOCK) + N_EXPERTS
    R = nblk * MOE_BLOCK
    n_used = (pad_end[-1] // MOE_BLOCK).astype(jnp.int32)
    blk = jnp.arange(nblk, dtype=jnp.int32)
    be = jnp.minimum(jnp.searchsorted(pad_end, blk * MOE_BLOCK, side='right'), N_EXPERTS - 1).astype(jnp.int32)
    be = jnp.where(blk < n_used, be, be[jnp.maximum(n_used - 1, 0)])
    src = jnp.zeros((R,), jnp.int32).at[dest].set(jnp.arange(A, dtype=jnp.int32) // TOP_K)
    xs = jnp.take(h_bf16, src, axis=0)
    ys = moe_experts(xs, be, n_used.reshape(1), w_gate, w_up, w_down, layer)
    d2 = dest.reshape(T, TOP_K)
    return gate[:, 0:1] * jnp.take(ys, d2[:, 0], axis=0) + gate[:, 1:2] * jnp.take(ys, d2[:, 1], axis=0)


def _rope_tables(L):
    half = A_HEAD_DIM // 2
    inv_freq = ROPE_BASE ** (-jnp.arange(0, half, 2, dtype=F32) / half)
    t = jnp.arange(L)
    row = (t // GRID_W).astype(F32)
    col = (t % GRID_W).astype(F32)
    ang = jnp.concatenate([row[:, None] * inv_freq[None, :], col[:, None] * inv_freq[None, :]], axis=-1)
    return jnp.cos(ang), jnp.sin(ang)


def _rope(t, cos, sin):
    q4 = A_HEAD_DIM // 4
    c = cos[None, :, None, :]
    s = sin[None, :, None, :]
    a1, a2, b1, b2 = t[..., :q4], t[..., q4:2 * q4], t[..., 2 * q4:3 * q4], t[..., 3 * q4:]
    cr, cc = c[..., :q4], c[..., q4:]
    sr, sc = s[..., :q4], s[..., q4:]
    return jnp.concatenate([a1 * cr - a2 * sr, a1 * sr + a2 * cr, b1 * cc - b2 * sc, b1 * sc + b2 * cc], axis=-1)


def _shift(u, mu):
    prev = jnp.pad(u, ((0, 0), (1, 0), (0, 0)))[:, :-1]
    nxt = jnp.pad(u, ((0, 0), (0, 1), (0, 0)))[:, 1:]
    return u + mu[0] * (prev - u) + mu[1] * (nxt - u)


def _heads(t, n):
    return t.reshape(t.shape[:-1] + (t.shape[-1] // n, n))


def _rwkv_inputs(u, w0, w_up, a0, a_up, k_k, k_a, rw):
    B, L, _ = u.shape
    off_w = 3 * rw
    off_a = off_w + 2 * R_LORA_W
    off_g = off_a + 2 * R_LORA_A
    r = u[..., :rw]
    k = u[..., rw:2 * rw]
    v = u[..., 2 * rw:3 * rw]
    wd = u[..., off_w:off_a].reshape(B, L, 2, R_LORA_W)
    ad = u[..., off_a:off_g].reshape(B, L, 2, R_LORA_A)
    hp = lax.Precision.HIGHEST
    w_log = w0[:, None, None, :] + jnp.einsum('bldr,drc->dblc', jnp.tanh(wd), w_up, precision=hp)
    log_decay = -jnp.exp(-jax.nn.softplus(-w_log) - 0.5)
    a = jax.nn.sigmoid(a0[:, None, None, :] + jnp.einsum('bldr,drc->dblc', ad, a_up, precision=hp))
    kk = _heads(k * k_k, R_HEAD)
    kk = kk * lax.rsqrt(jnp.maximum(jnp.sum(kk * kk, axis=-1, keepdims=True), 1e-24))
    kk = kk.reshape(B, L, rw)
    k_dir = k[None] * (1 + (a - 1) * k_a)
    return r, log_decay, k_dir, v, -kk, kk[None] * a, u[..., off_g:]


def _rwkv_mixer(uc, ul, w0, w_up, a0, a_up, g_up, k_k, k_a, r_k, lnx_w, lnx_b, want_ctx):
    B, Lc, _ = uc.shape
    Ll = ul.shape[1]
    rw = w0.shape[-1]
    H = rw // R_HEAD
    C = RWKV_CHUNK
    T = Lc + Ll
    pc = _rwkv_inputs(uc, w0, w_up, a0, a_up, k_k, k_a, rw)
    pL = _rwkv_inputs(ul, w0, w_up, a0, a_up, k_k, k_a, rw)

    def seq(xc, xl):
        xc = jnp.broadcast_to(xc, (2,) + xc.shape[-3:])
        xl = jnp.broadcast_to(xl, (2,) + xl.shape[-3:])
        f = jnp.concatenate([xc[0], xl[0]], axis=1)
        b = jnp.concatenate([jnp.flip(xc[1], axis=1), jnp.flip(xl[1], axis=1)], axis=1)
        return jnp.stack([f, b])

    r, lw, kd, v, a, b = (seq(pc[i], pL[i]) for i in range(6))
    nc = T // C
    lwc = lw.reshape(2, B, nc, C, rw)
    cum = jnp.cumsum(lwc, axis=3)
    g_in = jnp.exp(cum).reshape(2, B, T, rw)
    g_prev = jnp.exp(cum - lwc).reshape(2, B, T, rw)
    g_inv = jnp.exp(-cum).reshape(2, B, T, rw)
    g_end = jnp.exp(cum[:, :, :, -1, :])

    def lay(t):
        return t.reshape(2 * B, T, H, R_HEAD).transpose(0, 2, 1, 3)

    y = rwkv_chunk_scan(lay(a * g_prev), lay(b * g_inv), lay(kd * g_inv), lay(r * g_in), lay(v),
                        g_end.reshape(2 * B, nc, H, 1, R_HEAD))
    y = y.transpose(0, 2, 1, 3).reshape(2, B, T, rw)

    def readout(y2, p, L):
        r_, _, kd_, v_, _, _, gl = p
        ys = _heads(y2[0] + y2[1], R_HEAD)
        mean = jnp.mean(ys, axis=-1, keepdims=True)
        var = jnp.mean(jnp.square(ys - mean), axis=-1, keepdims=True)
        yn = ((ys - mean) * lax.rsqrt(var + R_GN_EPS)).reshape(B, L, rw) * lnx_w + lnx_b
        rk = _heads(r_[None] * kd_ * r_k.reshape(rw), R_HEAD)
        bonus = jnp.sum(jnp.sum(rk, axis=-1, keepdims=True) * _heads(v_, R_HEAD)[None], axis=0)
        g = jnp.dot(jax.nn.sigmoid(gl), g_up, precision=lax.Precision.HIGHEST)
        return (yn + bonus.reshape(B, L, rw)) * g

    y_lat = jnp.stack([y[0, :, Lc:], jnp.flip(y[1, :, Lc:], axis=1)])
    out_l = readout(y_lat, pL, Ll)
    if not want_ctx:
        return out_l, None
    y_ctx = jnp.stack([y[0, :, :Lc], jnp.flip(y[1, :, :Lc], axis=1)])
    return out_l, readout(y_ctx, pc, Lc)


def _dft_tables(L, ncols):
    N = 2 * L
    k = jnp.arange(L, dtype=jnp.int32)
    n = jnp.arange(ncols, dtype=jnp.int32)
    m = ((2 * k + 1)[:, None] * n[None, :]) % (2 * N)
    ang = m.astype(F32) * (math.pi / N)
    return jnp.cos(ang), -jnp.sin(ang)


def _hyena_filters(L, w1, b1, f1, w2, b2, f2, w3, width):
    hp = lax.Precision.HIGHEST
    t = jnp.linspace(0.0, 1.0, L, dtype=F32)[:, None]
    bands = jnp.linspace(1e-4, HY_BANDS - 1, HY_BANDS, dtype=F32)[None, :]
    ang = (2 * math.pi) * jnp.arange(L, dtype=F32)[:, None] / L * bands
    z = jnp.concatenate([t, jnp.cos(ang), -jnp.sin(ang)], axis=-1)
    h = jnp.sin(f1 * (jnp.dot(z, w1, precision=hp) + b1))
    h = jnp.sin(f2 * (jnp.dot(h, w2, precision=hp) + b2))
    h = jnp.dot(h, w3, precision=hp)
    deltas = jnp.abs(jnp.linspace(HY_MIN_DECAY, HY_MAX_DECAY, width, dtype=F32))
    h = h * jnp.exp(-t * jnp.tile(deltas, 2)[None, :])
    h_fwd, h_bwd = h[:, :width], h[:, width:]
    tail = jnp.flip(h_bwd[1:], axis=0)
    norm = (jnp.sum(jnp.abs(h_fwd), axis=0, keepdims=True) + jnp.sum(jnp.abs(tail), axis=0, keepdims=True))
    return jnp.concatenate([h_fwd, jnp.zeros((1, width), F32), -tail], axis=0) / norm


def _long_conv(u, k_signed):
    B, L, C = u.shape
    cos_f, nsin_f = _dft_tables(L, 2 * L)
    fwd_full = jnp.concatenate([cos_f, nsin_f], axis=0).astype(BF16)
    fwd = fwd_full[:, :L]
    inv = jnp.concatenate([cos_f[:, :L].T, nsin_f[:, :L].T], axis=1).astype(BF16)
    kf = mm(fwd_full, k_signed)
    kr, ki = kf[:L], kf[L:]
    ub = u.astype(BF16)
    outs = []
    for b in range(B):
        xf = mm(fwd, ub[b])
        xr, xi = xf[:L], xf[L:]
        yf = jnp.concatenate([xr * kr - xi * ki, xr * ki + xi * kr], axis=0)
        outs.append(mm(inv, yf.astype(BF16)) * (1.0 / L))
    return jnp.stack(outs)


def _conv3(u, w, b):
    up = jnp.pad(u, ((0, 0), (1, 1), (0, 0)))
    return up[:, :-2] * w[0] + up[:, 1:-1] * w[1] + up[:, 2:] * w[2] + b


def _hyena_core(z, conv_w, conv_b, k_signed, bias):
    z = _conv3(z, conv_w, conv_b)
    x0, x1, v = jnp.split(z, 3, axis=-1)
    u = v * x1
    y = _long_conv(u, k_signed) + u * bias
    return y * x0


def kernel(x, c, ctx, c_ctx, mod_w, mod_b, norm1_w, norm2_w, router_group_w, router_expert_w, expert_w_gate,
           expert_w_up, expert_w_down, mix_w_in, mix_w_out, attn_sink, rwkv_shift_mu, rwkv_w0, rwkv_w_up,
           rwkv_a0, rwkv_a_up, rwkv_g_up, rwkv_k_k, rwkv_k_a, rwkv_r_k, rwkv_lnx_w, rwkv_lnx_b, hy_w_in,
           hy_b_in, hy_conv_w, hy_conv_b, hy_ffn_w1, hy_ffn_b1, hy_sin_f1, hy_ffn_w2, hy_ffn_b2, hy_sin_f2,
           hy_ffn_w3, hy_bias, hy_w_out, final_norm_w):
    B, L, D = x.shape
    Cn = ctx.shape[1]
    depth = mod_w.shape[0]
    a_q = D // 2
    a_heads = a_q // A_HEAD_DIM
    a_kvh = a_heads // A_GROUP
    a_kv = a_kvh * A_HEAD_DIM
    a_cols = a_q + 2 * a_kv
    rw = D // 2
    in_cols = mix_w_in.shape[-1]
    in_pad = -(-in_cols // (2 * LANE)) * (2 * LANE)

    cc = jnp.concatenate([c, c_ctx[None], jnp.zeros((SUBLANE - B - 1, D), F32)], axis=0)
    sc = jax.nn.silu(cc)
    mods = [mm(sc, mod_w, w_idx=i, bias=mod_b[i]).reshape(SUBLANE, 6, D) for i in range(depth)]
    cos, sin = _rope_tables(L)

    xl = x.reshape(B * L, D)
    xc = ctx.reshape(B * Cn, D)
    for i in range(depth):
        ctx_later = any(j % 2 == 0 for j in range(i + 1, depth))
        ctx_here = ctx_later or (i % 2 == 0)
        ml_ = mods[i][:B]
        mc_ = jnp.broadcast_to(mods[i][B:B + 1], (B, 6, D))

        def nrm(m, gain, a, b):
            return ((gain * (1 + m[:, b]))[:, None, :], m[:, a][:, None, :])

        n1l = nrm(ml_, norm1_w[i], 0, 1)
        n1c = nrm(mc_, norm1_w[i], 0, 1)
        gate_l = ml_[:, 2][:, None, :]
        gate_c = mc_[:, 2][:, None, :]
        if i % 2 == 0:
            e = i // 2
            w_in = jnp.pad(mix_w_in[e], ((0, 0), (0, in_pad - in_cols))).astype(BF16)
            zl = mm(xl, w_in, norm=n1l, rows_per_group=L, tn=in_pad // 2)[:, :in_cols].reshape(B, L, in_cols)
            zc = mm(xc, w_in, norm=n1c, rows_per_group=Cn, tn=in_pad // 2)[:, :in_cols].reshape(B, Cn, in_cols)

            def kv(z, n):
                k = z[..., a_q:a_q + a_kv].reshape(B, n, a_kvh, A_HEAD_DIM)
                v = z[..., a_q + a_kv:a_cols].reshape(B, n, a_kvh, A_HEAD_DIM)
                return k, v

            scale = A_HEAD_DIM ** -0.5
            ql = _rope(zl[..., :a_q].reshape(B, L, a_heads, A_HEAD_DIM), cos, sin) * scale
            kl, vl = kv(zl, L)
            kl = _rope(kl, cos, sin)
            kc, vc = kv(zc, Cn)

            def hl_(t):
                return t.transpose(0, 2, 1, 3).astype(BF16)

            att_l = attention(hl_(ql), hl_(kl), hl_(vl), hl_(kc), hl_(vc), attn_sink[e], band=True)
            att_l = att_l.transpose(0, 2, 1, 3).reshape(B * L, a_q)
            ul = _shift(zl[..., a_cols:], rwkv_shift_mu[e])
            uc = _shift(zc[..., a_cols:], rwkv_shift_mu[e])
            rw_l, rw_c = _rwkv_mixer(uc, ul, rwkv_w0[e], rwkv_w_up[e], rwkv_a0[e], rwkv_a_up[e], rwkv_g_up[e],
                                     rwkv_k_k[e], rwkv_k_a[e], rwkv_r_k[e], rwkv_lnx_w[e], rwkv_lnx_b[e],
                                     ctx_later)
            w_out = mix_w_out[e].astype(BF16)
            cat_l = jnp.concatenate([att_l, rw_l.reshape(B * L, rw)], axis=-1)
            xl = mm(cat_l, w_out, res=(xl, gate_l), rows_per_group=L)
            if ctx_later:
                qc = zc[..., :a_q].reshape(B, Cn, a_heads, A_HEAD_DIM) * scale
                att_c = attention(hl_(qc), hl_(kc), hl_(vc), hl_(kc), hl_(vc), attn_sink[e], band=False)
                att_c = att_c.transpose(0, 2, 1, 3).reshape(B * Cn, a_q)
                cat_c = jnp.concatenate([att_c, rw_c.reshape(B * Cn, rw)], axis=-1)
                xc = mm(cat_c, w_out, res=(xc, gate_c), rows_per_group=Cn)
        else:
            o = i // 2
            w_in = hy_w_in[o].astype(BF16)
            w_out = hy_w_out[o].astype(BF16)
            fl = (hy_ffn_w1[o], hy_ffn_b1[o], hy_sin_f1[o], hy_ffn_w2[o], hy_ffn_b2[o], hy_sin_f2[o], hy_ffn_w3[o])
            zl = mm(xl, w_in, norm=n1l, bias=hy_b_in[o], rows_per_group=L).reshape(B, L, 3 * D)
            gl = _hyena_core(zl, hy_conv_w[o], hy_conv_b[o], _hyena_filters(L, *fl, D), hy_bias[o])
            xl = mm(gl.reshape(B * L, D), w_out, res=(xl, gate_l), rows_per_group=L)
            if ctx_later:
                zc = mm(xc, w_in, norm=n1c, bias=hy_b_in[o], rows_per_group=Cn).reshape(B, Cn, 3 * D)
                gc = _hyena_core(zc, hy_conv_w[o], hy_conv_b[o], _hyena_filters(Cn, *fl, D), hy_bias[o])
                xc = mm(gc.reshape(B * Cn, D), w_out, res=(xc, gate_c), rows_per_group=Cn)

        w_route = jnp.concatenate([router_group_w[i], router_expert_w[i]], axis=1)
        w_route = jnp.pad(w_route, ((0, 0), (0, LANE - w_route.shape[1])))
        n2l = nrm(ml_, norm2_w[i], 3, 4)
        lg_l, h_l = mm(xl, w_route, norm=n2l, rows_per_group=L, x3=True, emit_h=BF16)
        if ctx_later:
            n2c = nrm(mc_, norm2_w[i], 3, 4)
            lg_c, h_c = mm(xc, w_route, norm=n2c, rows_per_group=Cn, x3=True, emit_h=BF16)
            f = moe_layer(jnp.concatenate([h_l, h_c], axis=0), jnp.concatenate([lg_l, lg_c], axis=0),
                          expert_w_gate, expert_w_up, expert_w_down, i)
            xl = xl + jnp.repeat(ml_[:, 5], L, axis=0) * f[:B * L]
            xc = xc + jnp.repeat(mc_[:, 5], Cn, axis=0) * f[B * L:]
        else:
            f = moe_layer(h_l, lg_l, expert_w_gate, expert_w_up, expert_w_down, i)
            xl = xl + jnp.repeat(ml_[:, 5], L, axis=0) * f

    xf = xl.reshape(B, L, D)
    y = xf * lax.rsqrt(jnp.mean(xf * xf, axis=-1, keepdims=True) + NORM_EPS)
    return y * final_norm_w
```

```python
import functools
import math

import jax
import jax.numpy as jnp
from jax import lax
from jax.experimental import pallas as pl
from jax.experimental.pallas import tpu as pltpu

F32 = jnp.float32
BF16 = jnp.bfloat16

GRID_W = 64
NORM_EPS = 1e-6
A_HEAD_DIM = 64
A_GROUP = 8
A_WINDOW = 128
A_BLOCK = 128
ROPE_BASE = 10000.0
NEG_INF = -1e30
R_HEAD = 64
R_LORA_W = 64
R_LORA_A = 64
R_GN_EPS = 64e-5
HY_BANDS = 16
HY_MIN_DECAY = math.log(1e-2) / 1.5
HY_MAX_DECAY = math.log(1e-2) / 0.3
N_GROUPS = 4
EXPERTS_PER_GROUP = 8
N_EXPERTS = N_GROUPS * EXPERTS_PER_GROUP
TOP_K = 2

LANE = 128
SUBLANE = 8
VMEM_LIMIT = 56 * 1024 * 1024
MM_TILE_BYTES = 8 * 1024 * 1024
RWKV_CHUNK = 64
MOE_BLOCK = 256


def _pick(n, cands):
    for c in cands:
        if n % c == 0:
            return c
    return n


def _split_bf16(x):
    hi = x.astype(BF16)
    lo = (x - hi.astype(F32)).astype(BF16)
    return hi, lo


def _dg(a, b, dims):
    return lax.dot_general(a, b, dims, preferred_element_type=F32)


def _dot3(a, b, dims):
    ah, al = _split_bf16(a)
    bh, bl = _split_bf16(b)
    return _dg(ah, bh, dims) + (_dg(ah, bl, dims) + _dg(al, bh, dims))


def _mm_body(*refs, norm, bias, res, x3, emit_h, w3d, two_x):
    it = iter(refs)
    x_ref = next(it)
    x2_ref = next(it) if two_x else None
    w_ref = next(it)
    g_ref = next(it) if norm else None
    s_ref = next(it) if norm else None
    b_ref = next(it) if bias else None
    r_ref = next(it) if res else None
    gt_ref = next(it) if res else None
    o_ref = next(it)
    h_ref = next(it) if emit_h else None
    xh_ref = next(it)
    xl_ref = next(it) if x3 else None

    @pl.when(pl.program_id(1) == 0)
    def _():
        x = x_ref[...].astype(F32)
        if norm:
            ms = jnp.mean(x * x, axis=-1, keepdims=True)
            x = x * lax.rsqrt(ms + NORM_EPS) * g_ref[0] + s_ref[0]
        xh = x.astype(BF16)
        if two_x:
            xh_ref[:, :x.shape[1]] = xh
            xh_ref[:, x.shape[1]:] = x2_ref[...].astype(BF16)
        else:
            xh_ref[...] = xh
        if x3:
            xl_ref[...] = (x - xh.astype(F32)).astype(BF16)
        if emit_h:
            h_ref[...] = x.astype(h_ref.dtype)

    w = w_ref[0] if w3d else w_ref[...]
    wh = w.astype(BF16)
    acc = jnp.dot(xh_ref[...], wh, preferred_element_type=F32)
    if x3:
        wl = (w.astype(F32) - wh.astype(F32)).astype(BF16)
        acc = acc + (jnp.dot(xh_ref[...], wl, preferred_element_type=F32)
                     + jnp.dot(xl_ref[...], wh, preferred_element_type=F32))
    if bias:
        acc = acc + b_ref[...]
    if res:
        acc = r_ref[...] + gt_ref[0] * acc
    o_ref[...] = acc.astype(o_ref.dtype)


def _mm_tiles(M, K, N, rows_per_group, w_itemsize):
    tm = _pick(math.gcd(M, rows_per_group), (512, 256, 128, 64, 32, 16, 8))
    cands = [t for t in (2048, 1536, 1024, 768, 640, 512, 384, 256, 128)
             if K * t * w_itemsize <= MM_TILE_BYTES and tm * t * 4 <= MM_TILE_BYTES]
    tn = N if (N <= cands[0] and N % LANE == 0) else _pick(N, cands)
    return tm, tn


def mm(x, w, *, x2=None, w_idx=None, norm=None, bias=None, res=None, rows_per_group=None, x3=False,
       emit_h=None, out_dtype=F32, tn=None):
    M, K1 = x.shape
    K = K1 if x2 is None else K1 + x2.shape[1]
    assert x2 is None or (norm is None and not x3 and emit_h is None)
    N = w.shape[-1]
    if rows_per_group is None:
        rows_per_group = M
    tm, tn_auto = _mm_tiles(M, K, N, rows_per_group, w.dtype.itemsize)
    tn = tn_auto if tn is None else tn
    assert M % tm == 0 and N % tn == 0 and rows_per_group % tm == 0
    bpg = rows_per_group // tm

    if w_idx is None:
        w_spec = pl.BlockSpec((K, tn), lambda i, j: (0, j))
    else:
        w_spec = pl.BlockSpec((1, K, tn), lambda i, j: (w_idx, 0, j))
    in_specs = [pl.BlockSpec((tm, K1), lambda i, j: (i, 0))]
    args = [x]
    if x2 is not None:
        in_specs.append(pl.BlockSpec((tm, K - K1), lambda i, j: (i, 0)))
        args.append(x2)
    in_specs.append(w_spec)
    args.append(w)
    if norm is not None:
        in_specs += [pl.BlockSpec((1, 1, K), lambda i, j: (i // bpg, 0, 0))] * 2
        args += [norm[0], norm[1]]
    if bias is not None:
        in_specs.append(pl.BlockSpec((1, tn), lambda i, j: (0, j)))
        args.append(bias.reshape(1, N).astype(F32))
    if res is not None:
        in_specs.append(pl.BlockSpec((tm, tn), lambda i, j: (i, j)))
        in_specs.append(pl.BlockSpec((1, 1, tn), lambda i, j: (i // bpg, 0, j)))
        args += [res[0], res[1]]
    out_shape = [jax.ShapeDtypeStruct((M, N), out_dtype)]
    out_specs = [pl.BlockSpec((tm, tn), lambda i, j: (i, j))]
    if emit_h is not None:
        out_shape.append(jax.ShapeDtypeStruct((M, K), emit_h))
        out_specs.append(pl.BlockSpec((tm, K), lambda i, j: (i, 0)))
    scratch = [pltpu.VMEM((tm, K), BF16)]
    if x3:
        scratch.append(pltpu.VMEM((tm, K), BF16))
    body = functools.partial(_mm_body, norm=norm is not None, bias=bias is not None,
                             res=res is not None, x3=x3, emit_h=emit_h is not None,
                             w3d=w_idx is not None, two_x=x2 is not None)
    outs = pl.pallas_call(
        body,
        grid=(M // tm, N // tn),
        in_specs=in_specs,
        out_specs=out_specs,
        out_shape=out_shape,
        scratch_shapes=scratch,
        compiler_params=pltpu.CompilerParams(
            dimension_semantics=("parallel", "arbitrary"), vmem_limit_bytes=VMEM_LIMIT),
    )(*args)
    return outs if emit_h is not None else outs[0]


def _swap16(x):
    q4 = A_HEAD_DIM // 4
    lane = lax.broadcasted_iota(jnp.int32, x.shape, 1)
    return jnp.where((lane & q4) == 0, pltpu.roll(x, LANE - q4, axis=1), pltpu.roll(x, q4, axis=1))


def _rope_tile(x, cos, sin):
    parts = []
    for j in range(x.shape[1] // LANE):
        t = x[:, j * LANE:(j + 1) * LANE]
        parts.append(t * cos + _swap16(t) * sin)
    return parts[0] if len(parts) == 1 else jnp.concatenate(parts, axis=1)


def _attn_body(*refs, seq, n_kv, band):
    dh = A_HEAD_DIM
    if band:
        (sink_ref, q_ref, k_ref, v_ref, kc_ref, vc_ref, cq_ref, sq_ref, ck_ref, sk_ref,
         o_ref, qs_ref, ks_ref, vs_ref, kcs_ref, vcs_ref) = refs
    else:
        sink_ref, q_ref, kc_ref, vc_ref, o_ref, qs_ref, kcs_ref, vcs_ref = refs
    n = pl.program_id(1)
    scale = dh ** -0.5
    kcs_ref[...] = kc_ref[0].astype(BF16)
    vcs_ref[...] = vc_ref[0].astype(BF16)
    if band:
        win = 3 * A_BLOCK
        start = jnp.clip((n - 1) * A_BLOCK, 0, seq - win)
        start = pl.multiple_of(start, A_BLOCK)
        qpos = n * A_BLOCK + lax.broadcasted_iota(jnp.int32, (A_BLOCK, win), 0)
        kpos = start + lax.broadcasted_iota(jnp.int32, (A_BLOCK, win), 1)
        mask = jnp.abs(kpos - qpos) <= A_WINDOW
        qs_ref[...] = (_rope_tile(q_ref[0], cq_ref[...], sq_ref[...]) * scale).astype(BF16)
        ks_ref[...] = _rope_tile(k_ref[0, pl.ds(start, win), :], ck_ref[pl.ds(start, win), :],
                                 sk_ref[pl.ds(start, win), :]).astype(BF16)
        vs_ref[...] = v_ref[0, pl.ds(start, win), :].astype(BF16)
    else:
        qs_ref[...] = (q_ref[0] * scale).astype(BF16)
    nt = (((1,), (1,)), ((), ()))
    for h in range(n_kv):
        hs = slice(h * dh, (h + 1) * dh)
        kc = kcs_ref[:, hs]
        vc = vcs_ref[:, hs]
        if band:
            kw = ks_ref[:, hs]
            vw = vs_ref[:, hs]
        for g in range(A_GROUP):
            hd = h * A_GROUP + g
            q = qs_ref[:, hd * dh:(hd + 1) * dh]
            sink = sink_ref[hd]
            s_c = _dg(q, kc, nt)
            m = jnp.maximum(jnp.max(s_c, axis=-1, keepdims=True), sink)
            if band:
                s_w = jnp.where(mask, _dg(q, kw, nt), NEG_INF)
                m = jnp.maximum(m, jnp.max(s_w, axis=-1, keepdims=True))
                p_w = jnp.exp(s_w - m)
            p_c = jnp.exp(s_c - m)
            den = jnp.sum(p_c, axis=-1, keepdims=True) + jnp.exp(sink - m)
            o = jnp.dot(p_c.astype(BF16), vc, preferred_element_type=F32)
            if band:
                den = den + jnp.sum(p_w, axis=-1, keepdims=True)
                o = o + jnp.dot(p_w.astype(BF16), vw, preferred_element_type=F32)
            o_ref[0, :, hd * dh:(hd + 1) * dh] = o / den


def attention(z, zc, sink, rope, *, a_q, a_kv, band):
    src = z if band else zc
    B, L, _ = src.shape
    Cn = zc.shape[1]
    n_kv = a_kv // A_HEAD_DIM
    nb = L // A_BLOCK
    kblk = a_q // a_kv
    body = functools.partial(_attn_body, seq=L, n_kv=n_kv, band=band)
    q_spec = pl.BlockSpec((1, A_BLOCK, a_q), lambda b, n: (b, n, 0))
    kc_spec = pl.BlockSpec((1, Cn, a_kv), lambda b, n: (b, 0, kblk))
    vc_spec = pl.BlockSpec((1, Cn, a_kv), lambda b, n: (b, 0, kblk + 1))
    smem = pl.BlockSpec(memory_space=pltpu.SMEM)
    if band:
        cq, sq, ck, sk = rope
        tab_q = pl.BlockSpec((A_BLOCK, LANE), lambda b, n: (n, 0))
        tab_k = pl.BlockSpec((L, LANE), lambda b, n: (0, 0))
        in_specs = [smem, q_spec,
                    pl.BlockSpec((1, L, a_kv), lambda b, n: (b, 0, kblk)),
                    pl.BlockSpec((1, L, a_kv), lambda b, n: (b, 0, kblk + 1)),
                    kc_spec, vc_spec, tab_q, tab_q, tab_k, tab_k]
        args = (sink.astype(F32), z, z, z, zc, zc, cq, sq, ck, sk)
        scratch = [pltpu.VMEM((A_BLOCK, a_q), BF16), pltpu.VMEM((3 * A_BLOCK, a_kv), BF16),
                   pltpu.VMEM((3 * A_BLOCK, a_kv), BF16), pltpu.VMEM((Cn, a_kv), BF16),
                   pltpu.VMEM((Cn, a_kv), BF16)]
    else:
        in_specs = [smem, q_spec, kc_spec, vc_spec]
        args = (sink.astype(F32), zc, zc, zc)
        scratch = [pltpu.VMEM((A_BLOCK, a_q), BF16), pltpu.VMEM((Cn, a_kv), BF16), pltpu.VMEM((Cn, a_kv), BF16)]
    return pl.pallas_call(
        body,
        grid=(B, nb),
        in_specs=in_specs,
        out_specs=pl.BlockSpec((1, A_BLOCK, a_q), lambda b, n: (b, n, 0)),
        out_shape=jax.ShapeDtypeStruct((B, L, a_q), F32),
        scratch_shapes=scratch,
        compiler_params=pltpu.CompilerParams(
            dimension_semantics=("parallel", "arbitrary"), vmem_limit_bytes=VMEM_LIMIT),
    )(*args)


def _rwkv_body(u_ref, w0_ref, wup_ref, a0_ref, aup_ref, kk_ref, ka_ref, rk_ref, y_ref, bon_ref,
               st_ref, gp_s, gi_s, a_s, kn_s, kd_s, rt_s, at3, bt3, kt3, rt3, v3, gc3, *, rw):
    C = RWKV_CHUNK
    N = R_HEAD
    nh = rw // N
    d = pl.program_id(0)

    @pl.when(pl.program_id(2) == 0)
    def _():
        st_ref[...] = jnp.zeros_like(st_ref)

    row = lax.broadcasted_iota(jnp.int32, (C, C), 0)
    col = lax.broadcasted_iota(jnp.int32, (C, C), 1)
    rel = (row - col) * (1 - 2 * d)
    strict = (rel > 0)[None]
    incl = (rel >= 0)[None]
    eye = (row == col).astype(F32)[None]
    mm2 = (((1,), (0,)), ((), ()))

    lora_w = u_ref[0, :, 3 * rw:3 * rw + 2 * R_LORA_W]
    lora_a = u_ref[0, :, 3 * rw + 2 * R_LORA_W:3 * rw + 2 * R_LORA_W + 2 * R_LORA_A]
    w_log = w0_ref[0] + _dot3(jnp.tanh(lora_w), wup_ref[0], mm2)
    nx = -w_log
    softplus = jnp.maximum(nx, 0.0) + jnp.log(1.0 + jnp.exp(-jnp.abs(nx)))
    lw = -jnp.exp(-softplus - 0.5)
    a = jax.nn.sigmoid(a0_ref[0] + _dot3(lora_a, aup_ref[0], mm2))

    tri = jnp.where(rel >= 0, 1.0, 0.0).astype(BF16)
    l1 = lw.astype(BF16)
    r1 = lw - l1.astype(F32)
    l2 = r1.astype(BF16)
    l3 = (r1 - l2.astype(F32)).astype(BF16)
    cum = _dg(tri, l1, mm2) + (_dg(tri, l2, mm2) + _dg(tri, l3, mm2))
    g_end = jnp.exp(jnp.sum(lw, axis=0, keepdims=True))
    k = u_ref[0, :, rw:2 * rw]
    gi = jnp.exp(-cum)
    gp_s[...] = jnp.exp(cum - lw)
    gi_s[...] = gi
    a_s[...] = a
    kn_s[...] = k * kk_ref[...]
    kd = k * (1.0 + (a - 1.0) * ka_ref[...])
    kd_s[...] = kd
    rt_s[...] = u_ref[0, :, :rw] * jnp.exp(cum)

    for h in range(nh):
        sl = slice(h * N, (h + 1) * N)
        kk = kn_s[:, sl]
        kk = kk * lax.rsqrt(jnp.maximum(jnp.sum(kk * kk, axis=-1, keepdims=True), 1e-24))
        r_h = u_ref[0, :, h * N:(h + 1) * N]
        v_h = u_ref[0, :, 2 * rw + h * N:2 * rw + (h + 1) * N]
        kd_h = kd_s[:, sl]
        at3[h] = -kk * gp_s[:, sl]
        bt3[h] = kk * a_s[:, sl] * gi_s[:, sl]
        kt3[h] = kd_h * gi_s[:, sl]
        rt3[h] = rt_s[:, sl]
        v3[h] = v_h
        gc3[h] = g_end[:, sl]
        bon_ref[0, 0, :, sl] = jnp.sum(r_h * kd_h * rk_ref[:, sl], axis=-1, keepdims=True) * v_h

    at = at3[...]
    bt = bt3[...]
    kt = kt3[...]
    rt = rt3[...]
    v = v3[...]
    gc = gc3[...]
    m0 = st_ref[...]
    bmm = (((2,), (1,)), ((0,), (0,)))
    bmt = (((2,), (2,)), ((0,), (0,)))
    btm = (((1,), (1,)), ((0,), (0,)))

    ar = jnp.concatenate([at, rt], axis=1)
    bk = jnp.concatenate([bt, kt], axis=1)
    big = _dot3(ar, bk, bmt)
    a_ab = jnp.where(strict, big[:, :C, :C], 0.0)
    a_ak = jnp.where(strict, big[:, :C, C:], 0.0)
    a_rb = jnp.where(incl, big[:, C:, :C], 0.0)
    a_rk = jnp.where(incl, big[:, C:, C:], 0.0)

    tinv = eye + a_ab
    p = _dot3(a_ab, a_ab, bmm)
    for _ in range(int(math.log2(C)) - 2):
        both = _dot3(jnp.concatenate([tinv, p], axis=1), p, bmm)
        tinv = tinv + both[:, :C]
        p = both[:, C:]
    tinv = tinv + _dot3(tinv, p, bmm)

    av = _dot3(jnp.concatenate([a_ak, a_rk], axis=1), v, bmm)
    akv = av[:, :C]
    w12 = _dot3(tinv, jnp.concatenate([at, akv], axis=2), bmm)
    rb12 = _dot3(a_rb, w12, bmm)
    rq = rt + rb12[:, :, :at.shape[2]]
    yc = rb12[:, :, at.shape[2]:] + av[:, C:]
    w1 = w12[:, :, :at.shape[2]]
    w2 = w12[:, :, at.shape[2]:]

    um = _dot3(jnp.concatenate([w1, rq], axis=1), m0, bmt)
    u = um[:, :C] + w2
    y3 = um[:, C:] + yc
    for h in range(nh):
        y_ref[0, 0, :, h * N:(h + 1) * N] = y3[h]
    upd = _dot3(jnp.concatenate([u, v], axis=1), bk, btm)
    st_ref[...] = (m0 + upd) * gc


def rwkv_scan(u, n_ctx, w0, w_up, a0, a_up, k_k, k_a, r_k):
    B, T, cols = u.shape
    rw = w0.shape[-1]
    C = RWKV_CHUNK
    N = R_HEAD
    H = rw // N
    nc = T // C
    nctx = n_ctx // C

    def chunk(d, c):
        back = jnp.where(c < nctx, nctx - 1 - c, nc - 1 - c + nctx)
        return c + d * (back - c)

    def widen(w):
        z = jnp.zeros_like(w[0])
        return jnp.stack([jnp.concatenate([w[0], z], axis=0), jnp.concatenate([z, w[1]], axis=0)])

    dir_vec = pl.BlockSpec((1, 1, rw), lambda d, b, c: (d, 0, 0))
    dir_mat = pl.BlockSpec((1, 2 * R_LORA_W, rw), lambda d, b, c: (d, 0, 0))
    vec = pl.BlockSpec((1, rw), lambda d, b, c: (0, 0))
    out_spec = pl.BlockSpec((1, 1, C, rw), lambda d, b, c: (d, b, chunk(d, c), 0))
    full = pltpu.VMEM((C, rw), F32)
    per_head = pltpu.VMEM((H, C, N), F32)
    return pl.pallas_call(
        functools.partial(_rwkv_body, rw=rw),
        grid=(2, B, nc),
        in_specs=[pl.BlockSpec((1, C, cols), lambda d, b, c: (b, chunk(d, c), 0)),
                  dir_vec, dir_mat, dir_vec, dir_mat, vec, vec, vec],
        out_specs=[out_spec, out_spec],
        out_shape=[jax.ShapeDtypeStruct((2, B, T, rw), F32)] * 2,
        scratch_shapes=[pltpu.VMEM((H, N, N), F32), full, full, full, full, full, full,
                        per_head, per_head, per_head, per_head, per_head, pltpu.VMEM((H, 1, N), F32)],
        compiler_params=pltpu.CompilerParams(
            dimension_semantics=("parallel", "parallel", "arbitrary"), vmem_limit_bytes=VMEM_LIMIT),
    )(u, w0[:, None, :], widen(w_up), a0[:, None, :], widen(a_up),
      k_k.reshape(1, rw), k_a.reshape(1, rw), r_k.reshape(1, rw))


def _rwkv_readout_body(y_ref, bon_ref, u_ref, gup_ref, lw_ref, lb_ref, o_ref, *, rw):
    N = R_HEAD
    ys = y_ref[0, 0] + y_ref[1, 0]
    gate = _dot3(jax.nn.sigmoid(u_ref[0]), gup_ref[...], (((1,), (0,)), ((), ())))
    extra = bon_ref[0, 0] + bon_ref[1, 0] + lb_ref[...]
    for h in range(rw // N):
        sl = slice(h * N, (h + 1) * N)
        t = ys[:, sl]
        mean = jnp.mean(t, axis=-1, keepdims=True)
        var = jnp.mean(jnp.square(t - mean), axis=-1, keepdims=True)
        yn = (t - mean) * lax.rsqrt(var + R_GN_EPS) * lw_ref[:, sl]
        o_ref[0, :, sl] = (yn + extra[:, sl]) * gate[:, sl]


def rwkv_readout(y, bonus, u, g_up, lnx_w, lnx_b, g_off):
    _, B, T, rw = y.shape
    tm = _pick(T, (256, 128, 64))
    gw = 2 * LANE
    g_pad = jnp.pad(g_up, ((0, gw - g_up.shape[0]), (0, 0)))
    pair = pl.BlockSpec((2, 1, tm, rw), lambda b, i: (0, b, i, 0))
    vec = pl.BlockSpec((1, rw), lambda b, i: (0, 0))
    return pl.pallas_call(
        functools.partial(_rwkv_readout_body, rw=rw),
        grid=(B, T // tm),
        in_specs=[pair, pair, pl.BlockSpec((1, tm, gw), lambda b, i: (b, i, g_off // gw)),
                  pl.BlockSpec((gw, rw), lambda b, i: (0, 0)), vec, vec],
        out_specs=pl.BlockSpec((1, tm, rw), lambda b, i: (b, i, 0)),
        out_shape=jax.ShapeDtypeStruct((B, T, rw), F32),
        compiler_params=pltpu.CompilerParams(
            dimension_semantics=("parallel", "parallel"), vmem_limit_bytes=VMEM_LIMIT),
    )(y, bonus, u, g_pad, lnx_w.reshape(1, rw), lnx_b.reshape(1, rw))


def _moe_up_body(be_ref, nu_ref, x_ref, wg_ref, wu_ref, o_ref, wgs_ref, wus_ref):
    i = pl.program_id(0)
    prev = be_ref[jnp.maximum(i - 1, 0)]
    fresh = jnp.logical_or(i == 0, be_ref[i] != prev)

    @pl.when(fresh)
    def _():
        wgs_ref[...] = wg_ref[0, 0].astype(BF16)
        wus_ref[...] = wu_ref[0, 0].astype(BF16)

    @pl.when(i < nu_ref[0])
    def _():
        x = x_ref[...]
        g = jnp.dot(x, wgs_ref[...], preferred_element_type=F32)
        u = jnp.dot(x, wus_ref[...], preferred_element_type=F32)
        o_ref[...] = (g * jax.nn.sigmoid(g) * u).astype(o_ref.dtype)

    @pl.when(i >= nu_ref[0])
    def _():
        o_ref[...] = jnp.zeros_like(o_ref)


def _moe_down_body(be_ref, nu_ref, h_ref, wd_ref, o_ref, wds_ref):
    i = pl.program_id(0)
    prev = be_ref[jnp.maximum(i - 1, 0)]
    fresh = jnp.logical_or(i == 0, be_ref[i] != prev)

    @pl.when(fresh)
    def _():
        wds_ref[...] = wd_ref[0, 0].astype(BF16)

    @pl.when(i < nu_ref[0])
    def _():
        o_ref[...] = jnp.dot(h_ref[...], wds_ref[...], preferred_element_type=F32).astype(o_ref.dtype)

    @pl.when(i >= nu_ref[0])
    def _():
        o_ref[...] = jnp.zeros_like(o_ref)


def moe_experts(xs, block_expert, n_used, w_gate, w_up, w_down, layer):
    R, D = xs.shape
    Hd = w_gate.shape[-1]
    nblk = R // MOE_BLOCK
    hid = pl.pallas_call(
        _moe_up_body,
        grid_spec=pltpu.PrefetchScalarGridSpec(
            num_scalar_prefetch=2,
            grid=(nblk,),
            in_specs=[pl.BlockSpec((MOE_BLOCK, D), lambda i, be, nu: (i, 0)),
                      pl.BlockSpec((1, 1, D, Hd), lambda i, be, nu: (layer, be[i], 0, 0)),
                      pl.BlockSpec((1, 1, D, Hd), lambda i, be, nu: (layer, be[i], 0, 0))],
            out_specs=pl.BlockSpec((MOE_BLOCK, Hd), lambda i, be, nu: (i, 0)),
            scratch_shapes=[pltpu.VMEM((D, Hd), BF16), pltpu.VMEM((D, Hd), BF16)]),
        out_shape=jax.ShapeDtypeStruct((R, Hd), BF16),
        compiler_params=pltpu.CompilerParams(
            dimension_semantics=("arbitrary",), vmem_limit_bytes=VMEM_LIMIT),
    )(block_expert, n_used, xs, w_gate, w_up)
    return pl.pallas_call(
        _moe_down_body,
        grid_spec=pltpu.PrefetchScalarGridSpec(
            num_scalar_prefetch=2,
            grid=(nblk,),
            in_specs=[pl.BlockSpec((MOE_BLOCK, Hd), lambda i, be, nu: (i, 0)),
                      pl.BlockSpec((1, 1, Hd, D), lambda i, be, nu: (layer, be[i], 0, 0))],
            out_specs=pl.BlockSpec((MOE_BLOCK, D), lambda i, be, nu: (i, 0)),
            scratch_shapes=[pltpu.VMEM((Hd, D), BF16)]),
        out_shape=jax.ShapeDtypeStruct((R, D), F32),
        compiler_params=pltpu.CompilerParams(
            dimension_semantics=("arbitrary",), vmem_limit_bytes=VMEM_LIMIT),
    )(block_expert, n_used, hid, w_down)


def moe_layer(h_bf16, logits, w_gate, w_up, w_down, layer):
    T, D = h_bf16.shape
    lg = logits[:, :N_GROUPS]
    g_idx = jnp.argmax(lg, axis=-1)
    p_group = jnp.take_along_axis(jax.nn.softmax(lg, axis=-1), g_idx[:, None], axis=-1)
    le = logits[:, N_GROUPS:N_GROUPS + N_EXPERTS].reshape(T, N_GROUPS, EXPERTS_PER_GROUP)
    le_sel = jnp.take_along_axis(le, g_idx[:, None, None], axis=1)[:, 0]
    top_v, top_i = lax.top_k(le_sel, TOP_K)
    gate = p_group * jax.nn.softmax(top_v, axis=-1)
    expert = (g_idx[:, None] * EXPERTS_PER_GROUP + top_i).astype(jnp.int32)

    A = T * TOP_K
    e_flat = expert.reshape(A)
    onehot = (e_flat[:, None] == jnp.arange(N_EXPERTS, dtype=jnp.int32)[None, :]).astype(jnp.int32)
    rank = jnp.take_along_axis(jnp.cumsum(onehot, axis=0), e_flat[:, None], axis=1)[:, 0] - 1
    counts = jnp.sum(onehot, axis=0)
    padded = (counts + MOE_BLOCK - 1) // MOE_BLOCK * MOE_BLOCK
    pad_end = jnp.cumsum(padded)
    pad_start = pad_end - padded
    dest = pad_start[e_flat] + rank
    nblk = -(-A // MOE_BLOCK) + N_EXPERTS
    R = nblk * MOE_BLOCK
    n_used = (pad_end[-1] // MOE_BLOCK).astype(jnp.int32)
    blk = jnp.arange(nblk, dtype=jnp.int32)
    be = jnp.sum((pad_end[None, :] <= (blk * MOE_BLOCK)[:, None]).astype(jnp.int32), axis=1)
    be = jnp.minimum(be, N_EXPERTS - 1)
    be = jnp.where(blk < n_used, be, be[jnp.maximum(n_used - 1, 0)])
    src = jnp.zeros((R,), jnp.int32).at[dest].set(jnp.arange(A, dtype=jnp.int32) // TOP_K)
    xs = jnp.take(h_bf16, src, axis=0)
    ys = moe_experts(xs, be, n_used.reshape(1), w_gate, w_up, w_down, layer)
    d2 = dest.reshape(T, TOP_K)
    return gate[:, 0:1] * jnp.take(ys, d2[:, 0], axis=0) + gate[:, 1:2] * jnp.take(ys, d2[:, 1], axis=0)


def _rope_tables(L):
    half = A_HEAD_DIM // 2
    inv_freq = ROPE_BASE ** (-jnp.arange(0, half, 2, dtype=F32) / half)
    t = jnp.arange(L)
    row = (t // GRID_W).astype(F32)
    col = (t % GRID_W).astype(F32)
    ar = row[:, None] * inv_freq[None, :]
    ac = col[:, None] * inv_freq[None, :]
    cos = jnp.concatenate([jnp.cos(ar), jnp.cos(ar), jnp.cos(ac), jnp.cos(ac)], axis=-1)
    sin = jnp.concatenate([-jnp.sin(ar), jnp.sin(ar), -jnp.sin(ac), jnp.sin(ac)], axis=-1)
    reps = LANE // A_HEAD_DIM
    return jnp.tile(cos, (1, reps)), jnp.tile(sin, (1, reps))


def _shift(u, mu):
    prev = jnp.pad(u, ((0, 0), (1, 0), (0, 0)))[:, :-1]
    nxt = jnp.pad(u, ((0, 0), (0, 1), (0, 0)))[:, 1:]
    return u + mu[0] * (prev - u) + mu[1] * (nxt - u)


def _rwkv_mixer(zc, zl, off, mu, w0, w_up, a0, a_up, g_up, k_k, k_a, r_k, lnx_w, lnx_b):
    Lc = zc.shape[1]
    rw = w0.shape[-1]
    mu = jnp.pad(mu, ((0, 0), (0, zc.shape[-1] - off - mu.shape[1])))
    u = jnp.concatenate([_shift(zc[..., off:], mu), _shift(zl[..., off:], mu)], axis=1)
    y, bonus = rwkv_scan(u, Lc, w0, w_up, a0, a_up, k_k, k_a, r_k)
    g_off = 3 * rw + 2 * R_LORA_W + 2 * R_LORA_A
    return rwkv_readout(y, bonus, u, g_up, lnx_w, lnx_b, g_off)


def _dft_tables(L, ncols):
    N = 2 * L
    k = jnp.arange(L, dtype=jnp.int32)
    n = jnp.arange(ncols, dtype=jnp.int32)
    m = ((2 * k + 1)[:, None] * n[None, :]) % (2 * N)
    ang = m.astype(F32) * (math.pi / N)
    return jnp.cos(ang), -jnp.sin(ang)


def _hyena_filters(L, w1, b1, f1, w2, b2, f2, w3, width):
    hp = lax.Precision.HIGHEST
    t = jnp.linspace(0.0, 1.0, L, dtype=F32)[:, None]
    bands = jnp.linspace(1e-4, HY_BANDS - 1, HY_BANDS, dtype=F32)[None, :]
    ang = (2 * math.pi) * jnp.arange(L, dtype=F32)[:, None] / L * bands
    z = jnp.concatenate([t, jnp.cos(ang), -jnp.sin(ang)], axis=-1)
    h = jnp.sin(f1 * (jnp.dot(z, w1, precision=hp) + b1))
    h = jnp.sin(f2 * (jnp.dot(h, w2, precision=hp) + b2))
    h = jnp.dot(h, w3, precision=hp)
    deltas = jnp.abs(jnp.linspace(HY_MIN_DECAY, HY_MAX_DECAY, width, dtype=F32))
    h = h * jnp.exp(-t * jnp.tile(deltas, 2)[None, :])
    h_fwd, h_bwd = h[:, :width], h[:, width:]
    tail = jnp.flip(h_bwd[1:], axis=0)
    norm = (jnp.sum(jnp.abs(h_fwd), axis=0, keepdims=True) + jnp.sum(jnp.abs(tail), axis=0, keepdims=True))
    return jnp.concatenate([h_fwd, jnp.zeros((1, width), F32), -tail], axis=0) / norm


def _long_conv(u, k_signed):
    B, L, C = u.shape
    cos_f, nsin_f = _dft_tables(L, 2 * L)
    fwd_full = jnp.concatenate([cos_f, nsin_f], axis=0).astype(BF16)
    fwd = fwd_full[:, :L]
    inv = jnp.concatenate([cos_f[:, :L].T, nsin_f[:, :L].T], axis=1).astype(BF16)
    kf = mm(fwd_full, k_signed)
    kr, ki = kf[:L], kf[L:]
    ub = u.astype(BF16)
    outs = []
    for b in range(B):
        xf = mm(fwd, ub[b])
        xr, xi = xf[:L], xf[L:]
        yf = jnp.concatenate([xr * kr - xi * ki, xr * ki + xi * kr], axis=0)
        outs.append(mm(inv, yf.astype(BF16)) * (1.0 / L))
    return jnp.stack(outs)


def _conv3(u, w, b):
    up = jnp.pad(u, ((0, 0), (1, 1), (0, 0)))
    return up[:, :-2] * w[0] + up[:, 1:-1] * w[1] + up[:, 2:] * w[2] + b


def _hyena_core(z, conv_w, conv_b, k_signed, bias):
    z = _conv3(z, conv_w, conv_b)
    x0, x1, v = jnp.split(z, 3, axis=-1)
    u = v * x1
    y = _long_conv(u, k_signed) + u * bias
    return y * x0


def kernel(x, c, ctx, c_ctx, mod_w, mod_b, norm1_w, norm2_w, router_group_w, router_expert_w, expert_w_gate,
           expert_w_up, expert_w_down, mix_w_in, mix_w_out, attn_sink, rwkv_shift_mu, rwkv_w0, rwkv_w_up,
           rwkv_a0, rwkv_a_up, rwkv_g_up, rwkv_k_k, rwkv_k_a, rwkv_r_k, rwkv_lnx_w, rwkv_lnx_b, hy_w_in,
           hy_b_in, hy_conv_w, hy_conv_b, hy_ffn_w1, hy_ffn_b1, hy_sin_f1, hy_ffn_w2, hy_ffn_b2, hy_sin_f2,
           hy_ffn_w3, hy_bias, hy_w_out, final_norm_w):
    B, L, D = x.shape
    Cn = ctx.shape[1]
    depth = mod_w.shape[0]
    a_q = D // 2
    a_heads = a_q // A_HEAD_DIM
    a_kvh = a_heads // A_GROUP
    a_kv = a_kvh * A_HEAD_DIM
    a_cols = a_q + 2 * a_kv
    rw = D // 2
    in_cols = mix_w_in.shape[-1]
    in_pad = -(-in_cols // (2 * LANE)) * (2 * LANE)

    cc = jnp.concatenate([c, c_ctx[None], jnp.zeros((SUBLANE - B - 1, D), F32)], axis=0)
    sc = jax.nn.silu(cc)
    mods = [mm(sc, mod_w, w_idx=i, bias=mod_b[i]).reshape(SUBLANE, 6, D) for i in range(depth)]
    cos, sin = _rope_tables(L)
    rope = (cos, sin, cos, sin)

    xl = x.reshape(B * L, D)
    xc = ctx.reshape(B * Cn, D)
    for i in range(depth):
        ctx_later = any(j % 2 == 0 for j in range(i + 1, depth))
        ctx_here = ctx_later or (i % 2 == 0)
        ml_ = mods[i][:B]
        mc_ = jnp.broadcast_to(mods[i][B:B + 1], (B, 6, D))

        def nrm(m, gain, a, b):
            return ((gain * (1 + m[:, b]))[:, None, :], m[:, a][:, None, :])

        n1l = nrm(ml_, norm1_w[i], 0, 1)
        n1c = nrm(mc_, norm1_w[i], 0, 1)
        gate_l = ml_[:, 2][:, None, :]
        gate_c = mc_[:, 2][:, None, :]
        if i % 2 == 0:
            e = i // 2
            w_in = jnp.pad(mix_w_in[e], ((0, 0), (0, in_pad - in_cols))).astype(BF16)
            zl = mm(xl, w_in, norm=n1l, rows_per_group=L, tn=in_pad // 2).reshape(B, L, in_pad)
            zc = mm(xc, w_in, norm=n1c, rows_per_group=Cn, tn=in_pad // 2).reshape(B, Cn, in_pad)
            att_l = attention(zl, zc, attn_sink[e], rope, a_q=a_q, a_kv=a_kv, band=True)
            rw_all = _rwkv_mixer(zc, zl, a_cols, rwkv_shift_mu[e], rwkv_w0[e], rwkv_w_up[e], rwkv_a0[e],
                                 rwkv_a_up[e], rwkv_g_up[e], rwkv_k_k[e], rwkv_k_a[e], rwkv_r_k[e],
                                 rwkv_lnx_w[e], rwkv_lnx_b[e])
            w_out = mix_w_out[e].astype(BF16)
            xl = mm(att_l.reshape(B * L, a_q), w_out, x2=rw_all[:, Cn:].reshape(B * L, rw),
                    res=(xl, gate_l), rows_per_group=L)
            if ctx_later:
                att_c = attention(zl, zc, attn_sink[e], None, a_q=a_q, a_kv=a_kv, band=False)
                xc = mm(att_c.reshape(B * Cn, a_q), w_out, x2=rw_all[:, :Cn].reshape(B * Cn, rw),
                        res=(xc, gate_c), rows_per_group=Cn)
        else:
            o = i // 2
            w_in = hy_w_in[o].astype(BF16)
            w_out = hy_w_out[o].astype(BF16)
            fl = (hy_ffn_w1[o], hy_ffn_b1[o], hy_sin_f1[o], hy_ffn_w2[o], hy_ffn_b2[o], hy_sin_f2[o], hy_ffn_w3[o])
            zl = mm(xl, w_in, norm=n1l, bias=hy_b_in[o], rows_per_group=L).reshape(B, L, 3 * D)
            gl = _hyena_core(zl, hy_conv_w[o], hy_conv_b[o], _hyena_filters(L, *fl, D), hy_bias[o])
            xl = mm(gl.reshape(B * L, D), w_out, res=(xl, gate_l), rows_per_group=L)
            if ctx_later:
                zc = mm(xc, w_in, norm=n1c, bias=hy_b_in[o], rows_per_group=Cn).reshape(B, Cn, 3 * D)
                gc = _hyena_core(zc, hy_conv_w[o], hy_conv_b[o], _hyena_filters(Cn, *fl, D), hy_bias[o])
                xc = mm(gc.reshape(B * Cn, D), w_out, res=(xc, gate_c), rows_per_group=Cn)

        w_route = jnp.concatenate([router_group_w[i], router_expert_w[i]], axis=1)
        w_route = jnp.pad(w_route, ((0, 0), (0, LANE - w_route.shape[1])))
        n2l = nrm(ml_, norm2_w[i], 3, 4)
        lg_l, h_l = mm(xl, w_route, norm=n2l, rows_per_group=L, x3=True, emit_h=BF16)
        if ctx_later:
            n2c = nrm(mc_, norm2_w[i], 3, 4)
            lg_c, h_c = mm(xc, w_route, norm=n2c, rows_per_group=Cn, x3=True, emit_h=BF16)
            f = moe_layer(jnp.concatenate([h_l, h_c], axis=0), jnp.concatenate([lg_l, lg_c], axis=0),
                          expert_w_gate, expert_w_up, expert_w_down, i)
            xl = xl + jnp.repeat(ml_[:, 5], L, axis=0) * f[:B * L]
            xc = xc + jnp.repeat(mc_[:, 5], Cn, axis=0) * f[B * L:]
        else:
            f = moe_layer(h_l, lg_l, expert_w_gate, expert_w_up, expert_w_down, i)
            xl = xl + jnp.repeat(ml_[:, 5], L, axis=0) * f

    xf = xl.reshape(B, L, D)
    y = xf * lax.rsqrt(jnp.mean(xf * xf, axis=-1, keepdims=True) + NORM_EPS)
    return y * final_norm_w
```

```python
import functools
import math

import jax
import jax.numpy as jnp
from jax import lax
from jax.experimental import pallas as pl
from jax.experimental.pallas import tpu as pltpu

F32 = jnp.float32
BF16 = jnp.bfloat16

GRID_W = 64
NORM_EPS = 1e-6
A_HEAD_DIM = 64
A_GROUP = 8
A_WINDOW = 128
A_BLOCK = 128
ROPE_BASE = 10000.0
NEG_INF = -1e30
R_HEAD = 64
R_LORA_W = 64
R_LORA_A = 64
R_GN_EPS = 64e-5
HY_BANDS = 16
HY_MIN_DECAY = math.log(1e-2) / 1.5
HY_MAX_DECAY = math.log(1e-2) / 0.3
N_GROUPS = 4
EXPERTS_PER_GROUP = 8
N_EXPERTS = N_GROUPS * EXPERTS_PER_GROUP
TOP_K = 2

LANE = 128
SUBLANE = 8
VMEM_LIMIT = 56 * 1024 * 1024
MM_TILE_BYTES = 8 * 1024 * 1024
DFT_TILE = 512
RWKV_CHUNK = 64
MOE_BLOCK = 256


def _pick(n, cands):
    for c in cands:
        if n % c == 0:
            return c
    return n


def _split_bf16(x):
    hi = x.astype(BF16)
    lo = (x - hi.astype(F32)).astype(BF16)
    return hi, lo


def _dg(a, b, dims):
    return lax.dot_general(a, b, dims, preferred_element_type=F32)


def _dot3(a, b, dims):
    ah, al = _split_bf16(a)
    bh, bl = _split_bf16(b)
    ax = a.ndim - 2
    m = a.shape[ax]
    if dims[0][0][0] == ax:
        return _dg(ah, bh, dims) + (_dg(ah, bl, dims) + _dg(al, bh, dims))
    both = _dg(jnp.concatenate([ah, al], axis=ax), bh, dims)
    return (lax.slice_in_dim(both, 0, m, axis=ax) + lax.slice_in_dim(both, m, 2 * m, axis=ax)) + _dg(ah, bl, dims)


def _dot1(a, b, dims):
    return _dg(a.astype(BF16), b.astype(BF16), dims)


def _mm_body(*refs, norm, bias, res, x3, emit_h, w3d, two_x, cmul, post):
    it = iter(refs)
    x_ref = next(it)
    x2_ref = next(it) if two_x else None
    w_ref = next(it)
    g_ref = next(it) if norm else None
    s_ref = next(it) if norm else None
    b_ref = next(it) if bias else None
    r_ref = next(it) if res else None
    gt_ref = next(it) if res else None
    k_ref = next(it) if cmul else None
    pt_ref = next(it) if post else None
    px_ref = next(it) if post else None
    o_ref = next(it)
    h_ref = next(it) if emit_h else None
    xh_ref = next(it)
    xl_ref = next(it) if x3 else None

    @pl.when(pl.program_id(1) == 0)
    def _():
        x = x_ref[...].astype(F32)
        if norm:
            ms = jnp.mean(x * x, axis=-1, keepdims=True)
            x = x * lax.rsqrt(ms + NORM_EPS) * g_ref[0] + s_ref[0]
        xh = x.astype(BF16)
        if two_x:
            xh_ref[:, :x.shape[1]] = xh
            xh_ref[:, x.shape[1]:] = x2_ref[...].astype(BF16)
        else:
            xh_ref[...] = xh
        if x3:
            xl_ref[...] = (x - xh.astype(F32)).astype(BF16)
        if emit_h:
            h_ref[...] = x.astype(h_ref.dtype)

    w = w_ref[0] if w3d else w_ref[...]
    wh = w.astype(BF16)
    acc = jnp.dot(xh_ref[...], wh, preferred_element_type=F32)
    if x3:
        wl = (w.astype(F32) - wh.astype(F32)).astype(BF16)
        acc = acc + (jnp.dot(xh_ref[...], wl, preferred_element_type=F32)
                     + jnp.dot(xl_ref[...], wh, preferred_element_type=F32))
    if bias:
        acc = acc + b_ref[...]
    if res:
        acc = r_ref[...] + gt_ref[0] * acc
    if cmul:
        h = acc.shape[0] // 2
        re, im = acc[:h], acc[h:]
        kr, ki = k_ref[:h, :], k_ref[h:, :]
        acc = jnp.concatenate([re * kr - im * ki, re * ki + im * kr], axis=0)
    if post:
        acc = (acc * post + pt_ref[...]) * px_ref[...]
    o_ref[...] = acc.astype(o_ref.dtype)


def _mm_tiles(M, K, N, rows_per_group, w_itemsize):
    tm = _pick(math.gcd(M, rows_per_group), (512, 256, 128, 64, 32, 16, 8))
    cands = [t for t in (2048, 1536, 1024, 768, 640, 512, 384, 256, 128)
             if K * t * w_itemsize <= MM_TILE_BYTES and tm * t * 4 <= MM_TILE_BYTES]
    tn = N if (N <= cands[0] and N % LANE == 0) else _pick(N, cands)
    return tm, tn


def mm(x, w, *, x2=None, w_idx=None, norm=None, bias=None, res=None, rows_per_group=None, x3=False,
       emit_h=None, out_dtype=F32, tn=None, cmul=None, post=None):
    M, K1 = x.shape
    K = K1 if x2 is None else K1 + x2.shape[1]
    assert x2 is None or (norm is None and not x3 and emit_h is None)
    N = w.shape[-1]
    if rows_per_group is None:
        rows_per_group = M
    tm, tn_auto = _mm_tiles(M, K, N, rows_per_group, w.dtype.itemsize)
    tn = tn_auto if tn is None else tn
    assert M % tm == 0 and N % tn == 0 and rows_per_group % tm == 0
    bpg = rows_per_group // tm

    if w_idx is None:
        w_spec = pl.BlockSpec((K, tn), lambda i, j: (0, j))
    else:
        w_spec = pl.BlockSpec((1, K, tn), lambda i, j: (w_idx, 0, j))
    in_specs = [pl.BlockSpec((tm, K1), lambda i, j: (i, 0))]
    args = [x]
    if x2 is not None:
        in_specs.append(pl.BlockSpec((tm, K - K1), lambda i, j: (i, 0)))
        args.append(x2)
    in_specs.append(w_spec)
    args.append(w)
    if norm is not None:
        in_specs += [pl.BlockSpec((1, 1, K), lambda i, j: (i // bpg, 0, 0))] * 2
        args += [norm[0], norm[1]]
    if bias is not None:
        in_specs.append(pl.BlockSpec((1, tn), lambda i, j: (0, j)))
        args.append(bias.reshape(1, N).astype(F32))
    if res is not None:
        in_specs.append(pl.BlockSpec((tm, tn), lambda i, j: (i, j)))
        in_specs.append(pl.BlockSpec((1, 1, tn), lambda i, j: (i // bpg, 0, j)))
        args += [res[0], res[1]]
    if cmul is not None:
        in_specs.append(pl.BlockSpec((tm, tn), lambda i, j: (i, j)))
        args.append(cmul)
    if post is not None:
        in_specs += [pl.BlockSpec((tm, tn), lambda i, j: (i, j))] * 2
        args += [post[0], post[1]]
    out_shape = [jax.ShapeDtypeStruct((M, N), out_dtype)]
    out_specs = [pl.BlockSpec((tm, tn), lambda i, j: (i, j))]
    if emit_h is not None:
        out_shape.append(jax.ShapeDtypeStruct((M, K), emit_h))
        out_specs.append(pl.BlockSpec((tm, K), lambda i, j: (i, 0)))
    scratch = [pltpu.VMEM((tm, K), BF16)]
    if x3:
        scratch.append(pltpu.VMEM((tm, K), BF16))
    body = functools.partial(_mm_body, norm=norm is not None, bias=bias is not None,
                             res=res is not None, x3=x3, emit_h=emit_h is not None,
                             w3d=w_idx is not None, two_x=x2 is not None, cmul=cmul is not None,
                             post=None if post is None else post[2])
    outs = pl.pallas_call(
        body,
        grid=(M // tm, N // tn),
        in_specs=in_specs,
        out_specs=out_specs,
        out_shape=out_shape,
        scratch_shapes=scratch,
        compiler_params=pltpu.CompilerParams(
            dimension_semantics=("parallel", "arbitrary"), vmem_limit_bytes=VMEM_LIMIT),
    )(*args)
    return outs if emit_h is not None else outs[0]


def _swap16(x):
    q4 = A_HEAD_DIM // 4
    lane = lax.broadcasted_iota(jnp.int32, x.shape, 1)
    return jnp.where((lane & q4) == 0, pltpu.roll(x, LANE - q4, axis=1), pltpu.roll(x, q4, axis=1))


def _rope_tile(x, cos, sin):
    parts = []
    for j in range(x.shape[1] // LANE):
        t = x[:, j * LANE:(j + 1) * LANE]
        parts.append(t * cos + _swap16(t) * sin)
    return parts[0] if len(parts) == 1 else jnp.concatenate(parts, axis=1)


def _attn_body(*refs, seq, n_kv, band):
    dh = A_HEAD_DIM
    if band:
        (sink_ref, q_ref, k_ref, v_ref, kc_ref, vc_ref, cq_ref, sq_ref, ck_ref, sk_ref,
         o_ref, qs_ref, ks_ref, vs_ref, kcs_ref, vcs_ref) = refs
    else:
        sink_ref, q_ref, kc_ref, vc_ref, o_ref, qs_ref, kcs_ref, vcs_ref = refs
    n = pl.program_id(1)
    scale = dh ** -0.5
    kcs_ref[...] = kc_ref[0].astype(BF16)
    vcs_ref[...] = vc_ref[0].astype(BF16)
    if band:
        win = 3 * A_BLOCK
        start = jnp.clip((n - 1) * A_BLOCK, 0, seq - win)
        start = pl.multiple_of(start, A_BLOCK)
        qpos = n * A_BLOCK + lax.broadcasted_iota(jnp.int32, (A_BLOCK, win), 0)
        kpos = start + lax.broadcasted_iota(jnp.int32, (A_BLOCK, win), 1)
        mask = jnp.abs(kpos - qpos) <= A_WINDOW
        qs_ref[...] = (_rope_tile(q_ref[0], cq_ref[...], sq_ref[...]) * scale).astype(BF16)
        ks_ref[...] = _rope_tile(k_ref[0, pl.ds(start, win), :], ck_ref[pl.ds(start, win), :],
                                 sk_ref[pl.ds(start, win), :]).astype(BF16)
        vs_ref[...] = v_ref[0, pl.ds(start, win), :].astype(BF16)
    else:
        qs_ref[...] = (q_ref[0] * scale).astype(BF16)
    nt = (((1,), (1,)), ((), ()))
    for h in range(n_kv):
        hs = slice(h * dh, (h + 1) * dh)
        kc = kcs_ref[:, hs]
        vc = vcs_ref[:, hs]
        if band:
            kw = ks_ref[:, hs]
            vw = vs_ref[:, hs]
        for g in range(A_GROUP):
            hd = h * A_GROUP + g
            q = qs_ref[:, hd * dh:(hd + 1) * dh]
            sink = sink_ref[hd]
            s_c = _dg(q, kc, nt)
            m = jnp.maximum(jnp.max(s_c, axis=-1, keepdims=True), sink)
            if band:
                s_w = jnp.where(mask, _dg(q, kw, nt), NEG_INF)
                m = jnp.maximum(m, jnp.max(s_w, axis=-1, keepdims=True))
                p_w = jnp.exp(s_w - m)
            p_c = jnp.exp(s_c - m)
            den = jnp.sum(p_c, axis=-1, keepdims=True) + jnp.exp(sink - m)
            o = jnp.dot(p_c.astype(BF16), vc, preferred_element_type=F32)
            if band:
                den = den + jnp.sum(p_w, axis=-1, keepdims=True)
                o = o + jnp.dot(p_w.astype(BF16), vw, preferred_element_type=F32)
            o_ref[0, :, hd * dh:(hd + 1) * dh] = o / den


def attention(z, zc, sink, rope, *, a_q, a_kv, band):
    src = z if band else zc
    B, L, _ = src.shape
    Cn = zc.shape[1]
    n_kv = a_kv // A_HEAD_DIM
    nb = L // A_BLOCK
    kblk = a_q // a_kv
    body = functools.partial(_attn_body, seq=L, n_kv=n_kv, band=band)
    q_spec = pl.BlockSpec((1, A_BLOCK, a_q), lambda b, n: (b, n, 0))
    kc_spec = pl.BlockSpec((1, Cn, a_kv), lambda b, n: (b, 0, kblk))
    vc_spec = pl.BlockSpec((1, Cn, a_kv), lambda b, n: (b, 0, kblk + 1))
    smem = pl.BlockSpec(memory_space=pltpu.SMEM)
    if band:
        cq, sq, ck, sk = rope
        tab_q = pl.BlockSpec((A_BLOCK, LANE), lambda b, n: (n, 0))
        tab_k = pl.BlockSpec((L, LANE), lambda b, n: (0, 0))
        in_specs = [smem, q_spec,
                    pl.BlockSpec((1, L, a_kv), lambda b, n: (b, 0, kblk)),
                    pl.BlockSpec((1, L, a_kv), lambda b, n: (b, 0, kblk + 1)),
                    kc_spec, vc_spec, tab_q, tab_q, tab_k, tab_k]
        args = (sink.astype(F32), z, z, z, zc, zc, cq, sq, ck, sk)
        scratch = [pltpu.VMEM((A_BLOCK, a_q), BF16), pltpu.VMEM((3 * A_BLOCK, a_kv), BF16),
                   pltpu.VMEM((3 * A_BLOCK, a_kv), BF16), pltpu.VMEM((Cn, a_kv), BF16),
                   pltpu.VMEM((Cn, a_kv), BF16)]
    else:
        in_specs = [smem, q_spec, kc_spec, vc_spec]
        args = (sink.astype(F32), zc, zc, zc)
        scratch = [pltpu.VMEM((A_BLOCK, a_q), BF16), pltpu.VMEM((Cn, a_kv), BF16), pltpu.VMEM((Cn, a_kv), BF16)]
    return pl.pallas_call(
        body,
        grid=(B, nb),
        in_specs=in_specs,
        out_specs=pl.BlockSpec((1, A_BLOCK, a_q), lambda b, n: (b, n, 0)),
        out_shape=jax.ShapeDtypeStruct((B, L, a_q), F32),
        scratch_shapes=scratch,
        compiler_params=pltpu.CompilerParams(
            dimension_semantics=("parallel", "arbitrary"), vmem_limit_bytes=VMEM_LIMIT),
    )(*args)


def _rwkv_body(u_ref, w0_ref, wup_ref, a0_ref, aup_ref, kk_ref, ka_ref, rk_ref, y_ref, bon_ref,
               st_ref, gp_s, gi_s, a_s, kn_s, kd_s, rt_s, at3, bt3, kt3, rt3, v3, gc3, *, rw):
    C = RWKV_CHUNK
    N = R_HEAD
    nh = rw // N
    d = pl.program_id(0)

    @pl.when(pl.program_id(2) == 0)
    def _():
        st_ref[...] = jnp.zeros_like(st_ref)

    row = lax.broadcasted_iota(jnp.int32, (C, C), 0)
    col = lax.broadcasted_iota(jnp.int32, (C, C), 1)
    rel = (row - col) * (1 - 2 * d)
    strict = (rel > 0)[None]
    incl = (rel >= 0)[None]
    eye = (row == col).astype(F32)[None]
    mm2 = (((1,), (0,)), ((), ()))

    lora_w = u_ref[0, :, 3 * rw:3 * rw + 2 * R_LORA_W]
    lora_a = u_ref[0, :, 3 * rw + 2 * R_LORA_W:3 * rw + 2 * R_LORA_W + 2 * R_LORA_A]
    w_log = w0_ref[0] + _dot3(jnp.tanh(lora_w), wup_ref[0], mm2)
    nx = -w_log
    softplus = jnp.maximum(nx, 0.0) + jnp.log(1.0 + jnp.exp(-jnp.abs(nx)))
    lw = -jnp.exp(-softplus - 0.5)
    a = jax.nn.sigmoid(a0_ref[0] + _dot3(lora_a, aup_ref[0], mm2))

    tri = jnp.where(rel >= 0, 1.0, 0.0).astype(BF16)
    l1 = lw.astype(BF16)
    r1 = lw - l1.astype(F32)
    l2 = r1.astype(BF16)
    l3 = (r1 - l2.astype(F32)).astype(BF16)
    cum = _dg(tri, l1, mm2) + (_dg(tri, l2, mm2) + _dg(tri, l3, mm2))
    g_end = jnp.exp(jnp.sum(lw, axis=0, keepdims=True))
    k = u_ref[0, :, rw:2 * rw]
    gi = jnp.exp(-cum)
    gp_s[...] = jnp.exp(cum - lw)
    gi_s[...] = gi
    a_s[...] = a
    kn_s[...] = k * kk_ref[...]
    kd = k * (1.0 + (a - 1.0) * ka_ref[...])
    kd_s[...] = kd
    rt_s[...] = u_ref[0, :, :rw] * jnp.exp(cum)

    for h in range(nh):
        sl = slice(h * N, (h + 1) * N)
        kk = kn_s[:, sl]
        kk = kk * lax.rsqrt(jnp.maximum(jnp.sum(kk * kk, axis=-1, keepdims=True), 1e-24))
        r_h = u_ref[0, :, h * N:(h + 1) * N]
        v_h = u_ref[0, :, 2 * rw + h * N:2 * rw + (h + 1) * N]
        kd_h = kd_s[:, sl]
        at3[h] = -kk * gp_s[:, sl]
        bt3[h] = kk * a_s[:, sl] * gi_s[:, sl]
        kt3[h] = kd_h * gi_s[:, sl]
        rt3[h] = rt_s[:, sl]
        v3[h] = v_h
        gc3[h] = g_end[:, sl]
        bon_ref[0, 0, :, sl] = jnp.sum(r_h * kd_h * rk_ref[:, sl], axis=-1, keepdims=True) * v_h

    at = at3[...]
    bt = bt3[...]
    kt = kt3[...]
    rt = rt3[...]
    v = v3[...]
    gc = gc3[...]
    m0 = st_ref[...]
    bmm = (((2,), (1,)), ((0,), (0,)))
    bmt = (((2,), (2,)), ((0,), (0,)))
    btm = (((1,), (1,)), ((0,), (0,)))

    ar = jnp.concatenate([at, rt], axis=1)
    bk = jnp.concatenate([bt, kt], axis=1)
    big = _dot1(ar, bk, bmt)
    a_ab = jnp.where(strict, big[:, :C, :C], 0.0)
    a_ak = jnp.where(strict, big[:, :C, C:], 0.0)
    a_rb = jnp.where(incl, big[:, C:, :C], 0.0)
    a_rk = jnp.where(incl, big[:, C:, C:], 0.0)

    tinv = eye + a_ab
    p = _dot3(a_ab, a_ab, bmm)
    for _ in range(int(math.log2(C)) - 2):
        both = _dot3(jnp.concatenate([tinv, p], axis=1), p, bmm)
        tinv = tinv + both[:, :C]
        p = both[:, C:]
    tinv = tinv + _dot3(tinv, p, bmm)

    av = _dot1(jnp.concatenate([a_ak, a_rk], axis=1), v, bmm)
    akv = av[:, :C]
    w12 = _dot3(tinv, jnp.concatenate([at, akv], axis=2), bmm)
    rb12 = _dot1(a_rb, w12, bmm)
    rq = rt + rb12[:, :, :at.shape[2]]
    yc = rb12[:, :, at.shape[2]:] + av[:, C:]
    w1 = w12[:, :, :at.shape[2]]
    w2 = w12[:, :, at.shape[2]:]

    um = _dot1(jnp.concatenate([w1, rq], axis=1), m0, bmt)
    u = um[:, :C] + w2
    y3 = um[:, C:] + yc
    for h in range(nh):
        y_ref[0, 0, :, h * N:(h + 1) * N] = y3[h]
    upd = _dot1(jnp.concatenate([u, v], axis=1), bk, btm)
    st_ref[...] = (m0 + upd) * gc


def rwkv_scan(u, n_ctx, w0, w_up, a0, a_up, k_k, k_a, r_k):
    B, T, cols = u.shape
    rw = w0.shape[-1]
    C = RWKV_CHUNK
    N = R_HEAD
    H = rw // N
    nc = T // C
    nctx = n_ctx // C

    def chunk(d, c):
        back = jnp.where(c < nctx, nctx - 1 - c, nc - 1 - c + nctx)
        return c + d * (back - c)

    def widen(w):
        z = jnp.zeros_like(w[0])
        return jnp.stack([jnp.concatenate([w[0], z], axis=0), jnp.concatenate([z, w[1]], axis=0)])

    dir_vec = pl.BlockSpec((1, 1, rw), lambda d, b, c: (d, 0, 0))
    dir_mat = pl.BlockSpec((1, 2 * R_LORA_W, rw), lambda d, b, c: (d, 0, 0))
    vec = pl.BlockSpec((1, rw), lambda d, b, c: (0, 0))
    out_spec = pl.BlockSpec((1, 1, C, rw), lambda d, b, c: (d, b, chunk(d, c), 0))
    full = pltpu.VMEM((C, rw), F32)
    per_head = pltpu.VMEM((H, C, N), F32)
    return pl.pallas_call(
        functools.partial(_rwkv_body, rw=rw),
        grid=(2, B, nc),
        in_specs=[pl.BlockSpec((1, C, cols), lambda d, b, c: (b, chunk(d, c), 0)),
                  dir_vec, dir_mat, dir_vec, dir_mat, vec, vec, vec],
        out_specs=[out_spec, out_spec],
        out_shape=[jax.ShapeDtypeStruct((2, B, T, rw), F32)] * 2,
        scratch_shapes=[pltpu.VMEM((H, N, N), F32), full, full, full, full, full, full,
                        per_head, per_head, per_head, per_head, per_head, pltpu.VMEM((H, 1, N), F32)],
        compiler_params=pltpu.CompilerParams(
            dimension_semantics=("parallel", "parallel", "arbitrary"), vmem_limit_bytes=VMEM_LIMIT),
    )(u, w0[:, None, :], widen(w_up), a0[:, None, :], widen(a_up),
      k_k.reshape(1, rw), k_a.reshape(1, rw), r_k.reshape(1, rw))


def _rwkv_readout_body(y_ref, bon_ref, u_ref, gup_ref, lw_ref, lb_ref, o_ref, *, rw):
    N = R_HEAD
    ys = y_ref[0, 0] + y_ref[1, 0]
    gate = _dot3(jax.nn.sigmoid(u_ref[0]), gup_ref[...], (((1,), (0,)), ((), ())))
    extra = bon_ref[0, 0] + bon_ref[1, 0] + lb_ref[...]
    for h in range(rw // N):
        sl = slice(h * N, (h + 1) * N)
        t = ys[:, sl]
        mean = jnp.mean(t, axis=-1, keepdims=True)
        var = jnp.mean(jnp.square(t - mean), axis=-1, keepdims=True)
        yn = (t - mean) * lax.rsqrt(var + R_GN_EPS) * lw_ref[:, sl]
        o_ref[0, :, sl] = (yn + extra[:, sl]) * gate[:, sl]


def rwkv_readout(y, bonus, u, g_up, lnx_w, lnx_b, g_off):
    _, B, T, rw = y.shape
    tm = _pick(T, (256, 128, 64))
    gw = 2 * LANE
    g_pad = jnp.pad(g_up, ((0, gw - g_up.shape[0]), (0, 0)))
    pair = pl.BlockSpec((2, 1, tm, rw), lambda b, i: (0, b, i, 0))
    vec = pl.BlockSpec((1, rw), lambda b, i: (0, 0))
    return pl.pallas_call(
        functools.partial(_rwkv_readout_body, rw=rw),
        grid=(B, T // tm),
        in_specs=[pair, pair, pl.BlockSpec((1, tm, gw), lambda b, i: (b, i, g_off // gw)),
                  pl.BlockSpec((gw, rw), lambda b, i: (0, 0)), vec, vec],
        out_specs=pl.BlockSpec((1, tm, rw), lambda b, i: (b, i, 0)),
        out_shape=jax.ShapeDtypeStruct((B, T, rw), F32),
        compiler_params=pltpu.CompilerParams(
            dimension_semantics=("parallel", "parallel"), vmem_limit_bytes=VMEM_LIMIT),
    )(y, bonus, u, g_pad, lnx_w.reshape(1, rw), lnx_b.reshape(1, rw))


def _moe_up_body(be_ref, nu_ref, x_ref, wg_ref, wu_ref, o_ref, wgs_ref, wus_ref):
    i = pl.program_id(0)
    prev = be_ref[jnp.maximum(i - 1, 0)]
    fresh = jnp.logical_or(i == 0, be_ref[i] != prev)

    @pl.when(fresh)
    def _():
        wgs_ref[...] = wg_ref[0, 0].astype(BF16)
        wus_ref[...] = wu_ref[0, 0].astype(BF16)

    @pl.when(i < nu_ref[0])
    def _():
        x = x_ref[...]
        g = jnp.dot(x, wgs_ref[...], preferred_element_type=F32)
        u = jnp.dot(x, wus_ref[...], preferred_element_type=F32)
        o_ref[...] = (g * jax.nn.sigmoid(g) * u).astype(o_ref.dtype)

    @pl.when(i >= nu_ref[0])
    def _():
        o_ref[...] = jnp.zeros_like(o_ref)


def _moe_down_body(be_ref, nu_ref, h_ref, wd_ref, o_ref, wds_ref):
    i = pl.program_id(0)
    prev = be_ref[jnp.maximum(i - 1, 0)]
    fresh = jnp.logical_or(i == 0, be_ref[i] != prev)

    @pl.when(fresh)
    def _():
        wds_ref[...] = wd_ref[0, 0].astype(BF16)

    @pl.when(i < nu_ref[0])
    def _():
        o_ref[...] = jnp.dot(h_ref[...], wds_ref[...], preferred_element_type=F32).astype(o_ref.dtype)

    @pl.when(i >= nu_ref[0])
    def _():
        o_ref[...] = jnp.zeros_like(o_ref)


def moe_experts(xs, block_expert, n_used, w_gate, w_up, w_down, layer):
    R, D = xs.shape
    Hd = w_gate.shape[-1]
    nblk = R // MOE_BLOCK
    hid = pl.pallas_call(
        _moe_up_body,
        grid_spec=pltpu.PrefetchScalarGridSpec(
            num_scalar_prefetch=2,
            grid=(nblk,),
            in_specs=[pl.BlockSpec((MOE_BLOCK, D), lambda i, be, nu: (i, 0)),
                      pl.BlockSpec((1, 1, D, Hd), lambda i, be, nu: (layer, be[i], 0, 0)),
                      pl.BlockSpec((1, 1, D, Hd), lambda i, be, nu: (layer, be[i], 0, 0))],
            out_specs=pl.BlockSpec((MOE_BLOCK, Hd), lambda i, be, nu: (i, 0)),
            scratch_shapes=[pltpu.VMEM((D, Hd), BF16), pltpu.VMEM((D, Hd), BF16)]),
        out_shape=jax.ShapeDtypeStruct((R, Hd), BF16),
        compiler_params=pltpu.CompilerParams(
            dimension_semantics=("arbitrary",), vmem_limit_bytes=VMEM_LIMIT),
    )(block_expert, n_used, xs, w_gate, w_up)
    return pl.pallas_call(
        _moe_down_body,
        grid_spec=pltpu.PrefetchScalarGridSpec(
            num_scalar_prefetch=2,
            grid=(nblk,),
            in_specs=[pl.BlockSpec((MOE_BLOCK, Hd), lambda i, be, nu: (i, 0)),
                      pl.BlockSpec((1, 1, Hd, D), lambda i, be, nu: (layer, be[i], 0, 0))],
            out_specs=pl.BlockSpec((MOE_BLOCK, D), lambda i, be, nu: (i, 0)),
            scratch_shapes=[pltpu.VMEM((Hd, D), BF16)]),
        out_shape=jax.ShapeDtypeStruct((R, D), F32),
        compiler_params=pltpu.CompilerParams(
            dimension_semantics=("arbitrary",), vmem_limit_bytes=VMEM_LIMIT),
    )(block_expert, n_used, hid, w_down)


def moe_layer(h_bf16, logits, w_gate, w_up, w_down, layer):
    T, D = h_bf16.shape
    lg = logits[:, :N_GROUPS]
    g_idx = jnp.argmax(lg, axis=-1)
    p_group = jnp.take_along_axis(jax.nn.softmax(lg, axis=-1), g_idx[:, None], axis=-1)
    le = logits[:, N_GROUPS:N_GROUPS + N_EXPERTS].reshape(T, N_GROUPS, EXPERTS_PER_GROUP)
    le_sel = jnp.take_along_axis(le, g_idx[:, None, None], axis=1)[:, 0]
    top_v, top_i = lax.top_k(le_sel, TOP_K)
    gate = p_group * jax.nn.softmax(top_v, axis=-1)
    expert = (g_idx[:, None] * EXPERTS_PER_GROUP + top_i).astype(jnp.int32)

    A = T * TOP_K
    e_flat = expert.reshape(A)
    onehot = (e_flat[:, None] == jnp.arange(N_EXPERTS, dtype=jnp.int32)[None, :]).astype(jnp.int32)
    rank = jnp.take_along_axis(jnp.cumsum(onehot, axis=0), e_flat[:, None], axis=1)[:, 0] - 1
    counts = jnp.sum(onehot, axis=0)
    padded = (counts + MOE_BLOCK - 1) // MOE_BLOCK * MOE_BLOCK
    pad_end = jnp.cumsum(padded)
    pad_start = pad_end - padded
    dest = pad_start[e_flat] + rank
    nblk = -(-A // MOE_BLOCK) + N_EXPERTS
    R = nblk * MOE_BLOCK
    n_used = (pad_end[-1] // MOE_BLOCK).astype(jnp.int32)
    blk = jnp.arange(nblk, dtype=jnp.int32)
    be = jnp.sum((pad_end[None, :] <= (blk * MOE_BLOCK)[:, None]).astype(jnp.int32), axis=1)
    be = jnp.minimum(be, N_EXPERTS - 1)
    be = jnp.where(blk < n_used, be, be[jnp.maximum(n_used - 1, 0)])
    src = jnp.zeros((R,), jnp.int32).at[dest].set(jnp.arange(A, dtype=jnp.int32) // TOP_K)
    xs = jnp.take(h_bf16, src, axis=0, mode='clip')
    ys = moe_experts(xs, be, n_used.reshape(1), w_gate, w_up, w_down, layer)
    d2 = dest.reshape(T, TOP_K)
    return (gate[:, 0:1] * jnp.take(ys, d2[:, 0], axis=0, mode='clip')
            + gate[:, 1:2] * jnp.take(ys, d2[:, 1], axis=0, mode='clip'))


def _rope_tables(L):
    half = A_HEAD_DIM // 2
    inv_freq = ROPE_BASE ** (-jnp.arange(0, half, 2, dtype=F32) / half)
    t = jnp.arange(L)
    row = (t // GRID_W).astype(F32)
    col = (t % GRID_W).astype(F32)
    ar = row[:, None] * inv_freq[None, :]
    ac = col[:, None] * inv_freq[None, :]
    cos = jnp.concatenate([jnp.cos(ar), jnp.cos(ar), jnp.cos(ac), jnp.cos(ac)], axis=-1)
    sin = jnp.concatenate([-jnp.sin(ar), jnp.sin(ar), -jnp.sin(ac), jnp.sin(ac)], axis=-1)
    reps = LANE // A_HEAD_DIM
    return jnp.tile(cos, (1, reps)), jnp.tile(sin, (1, reps))


def _shift(u, mu):
    prev = jnp.pad(u, ((0, 0), (1, 0), (0, 0)))[:, :-1]
    nxt = jnp.pad(u, ((0, 0), (0, 1), (0, 0)))[:, 1:]
    return u + mu[0] * (prev - u) + mu[1] * (nxt - u)


def _rwkv_mixer(zc, zl, off, mu, w0, w_up, a0, a_up, g_up, k_k, k_a, r_k, lnx_w, lnx_b):
    Lc = zc.shape[1]
    rw = w0.shape[-1]
    mu = jnp.pad(mu, ((0, 0), (0, zc.shape[-1] - off - mu.shape[1])))
    u = jnp.concatenate([_shift(zc[..., off:], mu), _shift(zl[..., off:], mu)], axis=1)
    y, bonus = rwkv_scan(u, Lc, w0, w_up, a0, a_up, k_k, k_a, r_k)
    g_off = 3 * rw + 2 * R_LORA_W + 2 * R_LORA_A
    return rwkv_readout(y, bonus, u, g_up, lnx_w, lnx_b, g_off)


def _dft_table(L):
    N = 2 * L
    half = DFT_TILE // 2
    k = jnp.arange(L, dtype=jnp.int32)
    n = jnp.arange(L, dtype=jnp.int32)
    m = ((2 * k + 1)[:, None] * n[None, :]) % (2 * N)
    ang = m.astype(F32) * (math.pi / N)
    tab = jnp.stack([jnp.cos(ang).reshape(L // half, half, L), (-jnp.sin(ang)).reshape(L // half, half, L)], axis=1)
    return tab.reshape(2 * L, L)


def _hyena_filters(L, w1, b1, f1, w2, b2, f2, w3, width):
    hp = lax.Precision.HIGHEST
    t = jnp.linspace(0.0, 1.0, L, dtype=F32)[:, None]
    bands = jnp.linspace(1e-4, HY_BANDS - 1, HY_BANDS, dtype=F32)[None, :]
    ang = (2 * math.pi) * jnp.arange(L, dtype=F32)[:, None] / L * bands
    z = jnp.concatenate([t, jnp.cos(ang), -jnp.sin(ang)], axis=-1)
    h = jnp.sin(f1 * (jnp.dot(z, w1, precision=hp) + b1))
    h = jnp.sin(f2 * (jnp.dot(h, w2, precision=hp) + b2))
    h = jnp.dot(h, w3, precision=hp)
    deltas = jnp.abs(jnp.linspace(HY_MIN_DECAY, HY_MAX_DECAY, width, dtype=F32))
    h = h * jnp.exp(-t * jnp.tile(deltas, 2)[None, :])
    row0 = (jnp.arange(L) > 0).astype(F32)[:, None]
    h_bwd = h[:, width:] * row0
    h = jnp.concatenate([h[:, :width], h_bwd], axis=1)
    norm = jnp.sum(jnp.abs(h[:, :width]), axis=0, keepdims=True) + jnp.sum(jnp.abs(h_bwd), axis=0, keepdims=True)
    return h, norm


def _hy_pre_body(z_ref, zp_ref, zn_ref, w_ref, b_ref, hb_ref, u_ref, t_ref, x0_ref, *, width):
    i = pl.program_id(1)
    last = pl.num_programs(1) - 1
    z = z_ref[0]
    tl = z.shape[0]
    prev_row = jnp.where(i > 0, zp_ref[0, SUBLANE - 1:SUBLANE, :], 0.0)
    next_row = jnp.where(i < last, zn_ref[0, 0:1, :], 0.0)
    row = lax.broadcasted_iota(jnp.int32, (tl, 1), 0)
    up = jnp.where(row == 0, prev_row, pltpu.roll(z, 1, axis=0))
    dn = jnp.where(row == tl - 1, next_row, pltpu.roll(z, tl - 1, axis=0))
    c = up * w_ref[0:1, :] + z * w_ref[1:2, :] + dn * w_ref[2:3, :] + b_ref[...]
    u = c[:, 2 * width:] * c[:, width:2 * width]
    u_ref[0] = u.astype(u_ref.dtype)
    t_ref[0] = u * hb_ref[...]
    x0_ref[0] = c[:, :width]


def hyena_pre(z, conv_w, conv_b, bias):
    B, L, W3 = z.shape
    W = W3 // 3
    tl = _pick(L, (256, 128, 64, 32, 16, 8))
    nsub = tl // SUBLANE
    out = pl.BlockSpec((1, tl, W), lambda b, i: (b, i, 0))
    return pl.pallas_call(
        functools.partial(_hy_pre_body, width=W),
        grid=(B, L // tl),
        in_specs=[pl.BlockSpec((1, tl, W3), lambda b, i: (b, i, 0)),
                  pl.BlockSpec((1, SUBLANE, W3), lambda b, i: (b, jnp.maximum(i * nsub - 1, 0), 0)),
                  pl.BlockSpec((1, SUBLANE, W3), lambda b, i: (b, jnp.minimum((i + 1) * nsub, L // SUBLANE - 1), 0)),
                  pl.BlockSpec((3, W3), lambda b, i: (0, 0)),
                  pl.BlockSpec((1, W3), lambda b, i: (0, 0)),
                  pl.BlockSpec((1, W), lambda b, i: (0, 0))],
        out_specs=[out, out, out],
        out_shape=[jax.ShapeDtypeStruct((B, L, W), BF16), jax.ShapeDtypeStruct((B, L, W), F32),
                   jax.ShapeDtypeStruct((B, L, W), F32)],
        compiler_params=pltpu.CompilerParams(
            dimension_semantics=("parallel", "parallel"), vmem_limit_bytes=VMEM_LIMIT),
    )(z, z, z, conv_w, conv_b.reshape(1, W3), bias.reshape(1, W))


def _hyena_core(z, conv_w, conv_b, filt, bias):
    B, L, W3 = z.shape
    W = W3 // 3
    h, norm = filt
    u, t, x0 = hyena_pre(z, conv_w, conv_b, bias)
    tab = _dft_table(L)
    fwd = tab.astype(BF16)
    inv = tab.T.astype(BF16)
    hf = mm(fwd, h)
    half = DFT_TILE // 2
    sign = jnp.tile(jnp.concatenate([jnp.ones((half, 1), F32), -jnp.ones((half, 1), F32)]), (2 * L // DFT_TILE, 1))
    kf = (hf[:, :W] + sign * hf[:, W:]) / norm
    outs = []
    for b in range(B):
        yf = mm(fwd, u[b], cmul=kf, out_dtype=BF16)
        outs.append(mm(inv, yf, post=(t[b], x0[b], 1.0 / L)))
    return jnp.stack(outs)


def kernel(x, c, ctx, c_ctx, mod_w, mod_b, norm1_w, norm2_w, router_group_w, router_expert_w, expert_w_gate,
           expert_w_up, expert_w_down, mix_w_in, mix_w_out, attn_sink, rwkv_shift_mu, rwkv_w0, rwkv_w_up,
           rwkv_a0, rwkv_a_up, rwkv_g_up, rwkv_k_k, rwkv_k_a, rwkv_r_k, rwkv_lnx_w, rwkv_lnx_b, hy_w_in,
           hy_b_in, hy_conv_w, hy_conv_b, hy_ffn_w1, hy_ffn_b1, hy_sin_f1, hy_ffn_w2, hy_ffn_b2, hy_sin_f2,
           hy_ffn_w3, hy_bias, hy_w_out, final_norm_w):
    B, L, D = x.shape
    Cn = ctx.shape[1]
    depth = mod_w.shape[0]
    a_q = D // 2
    a_heads = a_q // A_HEAD_DIM
    a_kvh = a_heads // A_GROUP
    a_kv = a_kvh * A_HEAD_DIM
    a_cols = a_q + 2 * a_kv
    rw = D // 2
    in_cols = mix_w_in.shape[-1]
    in_pad = -(-in_cols // (2 * LANE)) * (2 * LANE)

    cc = jnp.concatenate([c, c_ctx[None], jnp.zeros((SUBLANE - B - 1, D), F32)], axis=0)
    sc = jax.nn.silu(cc)
    mods = [mm(sc, mod_w, w_idx=i, bias=mod_b[i]).reshape(SUBLANE, 6, D) for i in range(depth)]
    cos, sin = _rope_tables(L)
    rope = (cos, sin, cos, sin)

    xl = x.reshape(B * L, D)
    xc = ctx.reshape(B * Cn, D)
    for i in range(depth):
        ctx_later = any(j % 2 == 0 for j in range(i + 1, depth))
        ctx_here = ctx_later or (i % 2 == 0)
        ml_ = mods[i][:B]
        mc_ = jnp.broadcast_to(mods[i][B:B + 1], (B, 6, D))

        def nrm(m, gain, a, b):
            return ((gain * (1 + m[:, b]))[:, None, :], m[:, a][:, None, :])

        n1l = nrm(ml_, norm1_w[i], 0, 1)
        n1c = nrm(mc_, norm1_w[i], 0, 1)
        gate_l = ml_[:, 2][:, None, :]
        gate_c = mc_[:, 2][:, None, :]
        if i % 2 == 0:
            e = i // 2
            w_in = jnp.pad(mix_w_in[e], ((0, 0), (0, in_pad - in_cols))).astype(BF16)
            zl = mm(xl, w_in, norm=n1l, rows_per_group=L, tn=in_pad // 2).reshape(B, L, in_pad)
            zc = mm(xc, w_in, norm=n1c, rows_per_group=Cn, tn=in_pad // 2).reshape(B, Cn, in_pad)
            att_l = attention(zl, zc, attn_sink[e], rope, a_q=a_q, a_kv=a_kv, band=True)
            rw_all = _rwkv_mixer(zc, zl, a_cols, rwkv_shift_mu[e], rwkv_w0[e], rwkv_w_up[e], rwkv_a0[e],
                                 rwkv_a_up[e], rwkv_g_up[e], rwkv_k_k[e], rwkv_k_a[e], rwkv_r_k[e],
                                 rwkv_lnx_w[e], rwkv_lnx_b[e])
            w_out = mix_w_out[e].astype(BF16)
            xl = mm(att_l.reshape(B * L, a_q), w_out, x2=rw_all[:, Cn:].reshape(B * L, rw),
                    res=(xl, gate_l), rows_per_group=L)
            if ctx_later:
                att_c = attention(zl, zc, attn_sink[e], None, a_q=a_q, a_kv=a_kv, band=False)
                xc = mm(att_c.reshape(B * Cn, a_q), w_out, x2=rw_all[:, :Cn].reshape(B * Cn, rw),
                        res=(xc, gate_c), rows_per_group=Cn)
        else:
            o = i // 2
            w_in = hy_w_in[o].astype(BF16)
            w_out = hy_w_out[o].astype(BF16)
            fl = (hy_ffn_w1[o], hy_ffn_b1[o], hy_sin_f1[o], hy_ffn_w2[o], hy_ffn_b2[o], hy_sin_f2[o], hy_ffn_w3[o])
            zl = mm(xl, w_in, norm=n1l, bias=hy_b_in[o], rows_per_group=L).reshape(B, L, 3 * D)
            gl = _hyena_core(zl, hy_conv_w[o], hy_conv_b[o], _hyena_filters(L, *fl, D), hy_bias[o])
            xl = mm(gl.reshape(B * L, D), w_out, res=(xl, gate_l), rows_per_group=L)
            if ctx_later:
                zc = mm(xc, w_in, norm=n1c, bias=hy_b_in[o], rows_per_group=Cn).reshape(B, Cn, 3 * D)
                gc = _hyena_core(zc, hy_conv_w[o], hy_conv_b[o], _hyena_filters(Cn, *fl, D), hy_bias[o])
                xc = mm(gc.reshape(B * Cn, D), w_out, res=(xc, gate_c), rows_per_group=Cn)

        w_route = jnp.concatenate([router_group_w[i], router_expert_w[i]], axis=1)
        w_route = jnp.pad(w_route, ((0, 0), (0, LANE - w_route.shape[1])))
        n2l = nrm(ml_, norm2_w[i], 3, 4)
        lg_l, h_l = mm(xl, w_route, norm=n2l, rows_per_group=L, x3=True, emit_h=BF16)
        if ctx_later:
            n2c = nrm(mc_, norm2_w[i], 3, 4)
            lg_c, h_c = mm(xc, w_route, norm=n2c, rows_per_group=Cn, x3=True, emit_h=BF16)
            f = moe_layer(jnp.concatenate([h_l, h_c], axis=0), jnp.concatenate([lg_l, lg_c], axis=0),
                          expert_w_gate, expert_w_up, expert_w_down, i)
            xl = xl + jnp.repeat(ml_[:, 5], L, axis=0) * f[:B * L]
            xc = xc + jnp.repeat(mc_[:, 5], Cn, axis=0) * f[B * L:]
        else:
            f = moe_layer(h_l, lg_l, expert_w_gate, expert_w_up, expert_w_down, i)
            xl = xl + jnp.repeat(ml_[:, 5], L, axis=0) * f

    xf = xl.reshape(B, L, D)
    y = xf * lax.rsqrt(jnp.mean(xf * xf, axis=-1, keepdims=True) + NORM_EPS)
    return y * final_norm_w
```

```python
import functools
import math

import jax
import jax.numpy as jnp
from jax import lax
from jax.experimental import pallas as pl
from jax.experimental.pallas import tpu as pltpu

F32 = jnp.float32
BF16 = jnp.bfloat16

GRID_W = 64
NORM_EPS = 1e-6
A_HEAD_DIM = 64
A_GROUP = 8
A_WINDOW = 128
A_BLOCK = 128
ROPE_BASE = 10000.0
NEG_INF = -1e30
R_HEAD = 64
R_LORA_W = 64
R_LORA_A = 64
R_GN_EPS = 64e-5
HY_BANDS = 16
HY_MIN_DECAY = math.log(1e-2) / 1.5
HY_MAX_DECAY = math.log(1e-2) / 0.3
N_GROUPS = 4
EXPERTS_PER_GROUP = 8
N_EXPERTS = N_GROUPS * EXPERTS_PER_GROUP
TOP_K = 2

LANE = 128
SUBLANE = 8
VMEM_LIMIT = 56 * 1024 * 1024
MM_TILE_BYTES = 8 * 1024 * 1024
DFT_TILE = 512
RWKV_CHUNK = 64
MOE_BLOCK = 256


def _pick(n, cands):
    for c in cands:
        if n % c == 0:
            return c
    return n


def _split_bf16(x):
    hi = x.astype(BF16)
    lo = (x - hi.astype(F32)).astype(BF16)
    return hi, lo


def _dg(a, b, dims):
    return lax.dot_general(a, b, dims, preferred_element_type=F32)


def _dot3(a, b, dims):
    ah, al = _split_bf16(a)
    bh, bl = _split_bf16(b)
    ax = a.ndim - 2
    m = a.shape[ax]
    if dims[0][0][0] == ax:
        return _dg(ah, bh, dims) + (_dg(ah, bl, dims) + _dg(al, bh, dims))
    both = _dg(jnp.concatenate([ah, al], axis=ax), bh, dims)
    return (lax.slice_in_dim(both, 0, m, axis=ax) + lax.slice_in_dim(both, m, 2 * m, axis=ax)) + _dg(ah, bl, dims)


def _dot1(a, b, dims):
    return _dg(a.astype(BF16), b.astype(BF16), dims)


def _mm_body(*refs, norm, bias, res, x3, emit_h, w3d, two_x, cmul, post):
    it = iter(refs)
    x_ref = next(it)
    x2_ref = next(it) if two_x else None
    w_ref = next(it)
    g_ref = next(it) if norm else None
    s_ref = next(it) if norm else None
    b_ref = next(it) if bias else None
    r_ref = next(it) if res else None
    gt_ref = next(it) if res else None
    k_ref = next(it) if cmul else None
    pt_ref = next(it) if post else None
    px_ref = next(it) if post else None
    o_ref = next(it)
    h_ref = next(it) if emit_h else None
    xh_ref = next(it)
    xl_ref = next(it) if x3 else None

    @pl.when(pl.program_id(1) == 0)
    def _():
        x = x_ref[...].astype(F32)
        if norm:
            ms = jnp.mean(x * x, axis=-1, keepdims=True)
            x = x * lax.rsqrt(ms + NORM_EPS) * g_ref[0] + s_ref[0]
        xh = x.astype(BF16)
        if two_x:
            xh_ref[:, :x.shape[1]] = xh
            xh_ref[:, x.shape[1]:] = x2_ref[...].astype(BF16)
        else:
            xh_ref[...] = xh
        if x3:
            xl_ref[...] = (x - xh.astype(F32)).astype(BF16)
        if emit_h:
            h_ref[...] = x.astype(h_ref.dtype)

    w = w_ref[0] if w3d else w_ref[...]
    wh = w.astype(BF16)
    acc = jnp.dot(xh_ref[...], wh, preferred_element_type=F32)
    if x3:
        wl = (w.astype(F32) - wh.astype(F32)).astype(BF16)
        acc = acc + (jnp.dot(xh_ref[...], wl, preferred_element_type=F32)
                     + jnp.dot(xl_ref[...], wh, preferred_element_type=F32))
    if bias:
        acc = acc + b_ref[...]
    if res:
        acc = r_ref[...] + gt_ref[0] * acc
    if cmul:
        h = acc.shape[0] // 2
        re, im = acc[:h], acc[h:]
        kr, ki = k_ref[:h, :], k_ref[h:, :]
        acc = jnp.concatenate([re * kr - im * ki, re * ki + im * kr], axis=0)
    if post:
        acc = (acc * post + pt_ref[...]) * px_ref[...]
    o_ref[...] = acc.astype(o_ref.dtype)


def _mm_tiles(M, K, N, rows_per_group, w_itemsize):
    tm = _pick(math.gcd(M, rows_per_group), (512, 256, 128, 64, 32, 16, 8))
    cands = [t for t in (2048, 1536, 1024, 768, 640, 512, 384, 256, 128)
             if K * t * w_itemsize <= MM_TILE_BYTES and tm * t * 4 <= MM_TILE_BYTES]
    tn = N if (N <= cands[0] and N % LANE == 0) else _pick(N, cands)
    return tm, tn


def mm(x, w, *, x2=None, w_idx=None, norm=None, bias=None, res=None, rows_per_group=None, x3=False,
       emit_h=None, out_dtype=F32, tn=None, cmul=None, post=None):
    M, K1 = x.shape
    K = K1 if x2 is None else K1 + x2.shape[1]
    assert x2 is None or (norm is None and not x3 and emit_h is None)
    N = w.shape[-1]
    if rows_per_group is None:
        rows_per_group = M
    tm, tn_auto = _mm_tiles(M, K, N, rows_per_group, w.dtype.itemsize)
    tn = tn_auto if tn is None else tn
    assert M % tm == 0 and N % tn == 0 and rows_per_group % tm == 0
    bpg = rows_per_group // tm

    if w_idx is None:
        w_spec = pl.BlockSpec((K, tn), lambda i, j: (0, j))
    else:
        w_spec = pl.BlockSpec((1, K, tn), lambda i, j: (w_idx, 0, j))
    in_specs = [pl.BlockSpec((tm, K1), lambda i, j: (i, 0))]
    args = [x]
    if x2 is not None:
        in_specs.append(pl.BlockSpec((tm, K - K1), lambda i, j: (i, 0)))
        args.append(x2)
    in_specs.append(w_spec)
    args.append(w)
    if norm is not None:
        in_specs += [pl.BlockSpec((1, 1, K), lambda i, j: (i // bpg, 0, 0))] * 2
        args += [norm[0], norm[1]]
    if bias is not None:
        in_specs.append(pl.BlockSpec((1, tn), lambda i, j: (0, j)))
        args.append(bias.reshape(1, N).astype(F32))
    if res is not None:
        in_specs.append(pl.BlockSpec((tm, tn), lambda i, j: (i, j)))
        in_specs.append(pl.BlockSpec((1, 1, tn), lambda i, j: (i // bpg, 0, j)))
        args += [res[0], res[1]]
    if cmul is not None:
        in_specs.append(pl.BlockSpec((tm, tn), lambda i, j: (i, j)))
        args.append(cmul)
    if post is not None:
        in_specs += [pl.BlockSpec((tm, tn), lambda i, j: (i, j))] * 2
        args += [post[0], post[1]]
    out_shape = [jax.ShapeDtypeStruct((M, N), out_dtype)]
    out_specs = [pl.BlockSpec((tm, tn), lambda i, j: (i, j))]
    if emit_h is not None:
        out_shape.append(jax.ShapeDtypeStruct((M, K), emit_h))
        out_specs.append(pl.BlockSpec((tm, K), lambda i, j: (i, 0)))
    scratch = [pltpu.VMEM((tm, K), BF16)]
    if x3:
        scratch.append(pltpu.VMEM((tm, K), BF16))
    body = functools.partial(_mm_body, norm=norm is not None, bias=bias is not None,
                             res=res is not None, x3=x3, emit_h=emit_h is not None,
                             w3d=w_idx is not None, two_x=x2 is not None, cmul=cmul is not None,
                             post=None if post is None else post[2])
    outs = pl.pallas_call(
        body,
        grid=(M // tm, N // tn),
        in_specs=in_specs,
        out_specs=out_specs,
        out_shape=out_shape,
        scratch_shapes=scratch,
        compiler_params=pltpu.CompilerParams(
            dimension_semantics=("parallel", "arbitrary"), vmem_limit_bytes=VMEM_LIMIT),
    )(*args)
    return outs if emit_h is not None else outs[0]


def _swap16(x):
    q4 = A_HEAD_DIM // 4
    lane = lax.broadcasted_iota(jnp.int32, x.shape, 1)
    return jnp.where((lane & q4) == 0, pltpu.roll(x, LANE - q4, axis=1), pltpu.roll(x, q4, axis=1))


def _rope_tile(x, cos, sin):
    parts = []
    for j in range(x.shape[1] // LANE):
        t = x[:, j * LANE:(j + 1) * LANE]
        parts.append(t * cos + _swap16(t) * sin)
    return parts[0] if len(parts) == 1 else jnp.concatenate(parts, axis=1)


def _attn_body(*refs, seq, n_kv, band):
    dh = A_HEAD_DIM
    if band:
        (sink_ref, q_ref, k_ref, v_ref, kc_ref, vc_ref, cq_ref, sq_ref, ck_ref, sk_ref,
         o_ref, qs_ref, ks_ref, vs_ref, kcs_ref, vcs_ref) = refs
    else:
        sink_ref, q_ref, kc_ref, vc_ref, o_ref, qs_ref, kcs_ref, vcs_ref = refs
    n = pl.program_id(1)
    scale = dh ** -0.5
    kcs_ref[...] = kc_ref[0].astype(BF16)
    vcs_ref[...] = vc_ref[0].astype(BF16)
    if band:
        win = 3 * A_BLOCK
        start = jnp.clip((n - 1) * A_BLOCK, 0, seq - win)
        start = pl.multiple_of(start, A_BLOCK)
        qpos = n * A_BLOCK + lax.broadcasted_iota(jnp.int32, (A_BLOCK, win), 0)
        kpos = start + lax.broadcasted_iota(jnp.int32, (A_BLOCK, win), 1)
        mask = jnp.abs(kpos - qpos) <= A_WINDOW
        qs_ref[...] = (_rope_tile(q_ref[0], cq_ref[...], sq_ref[...]) * scale).astype(BF16)
        ks_ref[...] = _rope_tile(k_ref[0, pl.ds(start, win), :], ck_ref[pl.ds(start, win), :],
                                 sk_ref[pl.ds(start, win), :]).astype(BF16)
        vs_ref[...] = v_ref[0, pl.ds(start, win), :].astype(BF16)
    else:
        qs_ref[...] = (q_ref[0] * scale).astype(BF16)
    nt = (((1,), (1,)), ((), ()))
    for h in range(n_kv):
        hs = slice(h * dh, (h + 1) * dh)
        kc = kcs_ref[:, hs]
        vc = vcs_ref[:, hs]
        if band:
            kw = ks_ref[:, hs]
            vw = vs_ref[:, hs]
        for g in range(A_GROUP):
            hd = h * A_GROUP + g
            q = qs_ref[:, hd * dh:(hd + 1) * dh]
            sink = sink_ref[hd]
            s_c = _dg(q, kc, nt)
            m = jnp.maximum(jnp.max(s_c, axis=-1, keepdims=True), sink)
            if band:
                s_w = jnp.where(mask, _dg(q, kw, nt), NEG_INF)
                m = jnp.maximum(m, jnp.max(s_w, axis=-1, keepdims=True))
                p_w = jnp.exp(s_w - m)
            p_c = jnp.exp(s_c - m)
            den = jnp.sum(p_c, axis=-1, keepdims=True) + jnp.exp(sink - m)
            o = jnp.dot(p_c.astype(BF16), vc, preferred_element_type=F32)
            if band:
                den = den + jnp.sum(p_w, axis=-1, keepdims=True)
                o = o + jnp.dot(p_w.astype(BF16), vw, preferred_element_type=F32)
            o_ref[0, :, hd * dh:(hd + 1) * dh] = o / den


def attention(z, zc, sink, rope, *, a_q, a_kv, q_off, band):
    src = z if band else zc
    B, L, _ = src.shape
    Cn = zc.shape[1]
    n_kv = a_kv // A_HEAD_DIM
    nb = L // A_BLOCK
    qblk = q_off // a_q
    kblk = (q_off + a_q) // a_kv
    body = functools.partial(_attn_body, seq=L, n_kv=n_kv, band=band)
    q_spec = pl.BlockSpec((1, A_BLOCK, a_q), lambda b, n: (b, n, qblk))
    kc_spec = pl.BlockSpec((1, Cn, a_kv), lambda b, n: (b, 0, kblk))
    vc_spec = pl.BlockSpec((1, Cn, a_kv), lambda b, n: (b, 0, kblk + 1))
    smem = pl.BlockSpec(memory_space=pltpu.SMEM)
    if band:
        cq, sq, ck, sk = rope
        tab_q = pl.BlockSpec((A_BLOCK, LANE), lambda b, n: (n, 0))
        tab_k = pl.BlockSpec((L, LANE), lambda b, n: (0, 0))
        in_specs = [smem, q_spec,
                    pl.BlockSpec((1, L, a_kv), lambda b, n: (b, 0, kblk)),
                    pl.BlockSpec((1, L, a_kv), lambda b, n: (b, 0, kblk + 1)),
                    kc_spec, vc_spec, tab_q, tab_q, tab_k, tab_k]
        args = (sink.astype(F32), z, z, z, zc, zc, cq, sq, ck, sk)
        scratch = [pltpu.VMEM((A_BLOCK, a_q), BF16), pltpu.VMEM((3 * A_BLOCK, a_kv), BF16),
                   pltpu.VMEM((3 * A_BLOCK, a_kv), BF16), pltpu.VMEM((Cn, a_kv), BF16),
                   pltpu.VMEM((Cn, a_kv), BF16)]
    else:
        in_specs = [smem, q_spec, kc_spec, vc_spec]
        args = (sink.astype(F32), zc, zc, zc)
        scratch = [pltpu.VMEM((A_BLOCK, a_q), BF16), pltpu.VMEM((Cn, a_kv), BF16), pltpu.VMEM((Cn, a_kv), BF16)]
    return pl.pallas_call(
        body,
        grid=(B, nb),
        in_specs=in_specs,
        out_specs=pl.BlockSpec((1, A_BLOCK, a_q), lambda b, n: (b, n, 0)),
        out_shape=jax.ShapeDtypeStruct((B, L, a_q), F32),
        scratch_shapes=scratch,
        compiler_params=pltpu.CompilerParams(
            dimension_semantics=("parallel", "arbitrary"), vmem_limit_bytes=VMEM_LIMIT),
    )(*args)


def _rwkv_chunk_index(d, c, nctx, nc):
    back = jnp.where(c < nctx, nctx - 1 - c, nc - 1 - c + nctx)
    return c + d * (back - c)


def _rwkv_body(zc_ref, zcp_ref, zcn_ref, zl_ref, zlp_ref, zln_ref, mu_ref, w0_ref, wup_ref, a0_ref, aup_ref,
               kk_ref, ka_ref, rk_ref, y_ref, bon_ref, gs_ref,
               st_ref, gp_s, gi_s, a_s, kn_s, kd_s, rt_s, at3, bt3, kt3, rt3, v3, gc3, u_ref, *, rw, nctx, g_off):
    C = RWKV_CHUNK
    N = R_HEAD
    nh = rw // N
    d = pl.program_id(0)
    nc = pl.num_programs(2)

    @pl.when(pl.program_id(2) == 0)
    def _():
        st_ref[...] = jnp.zeros_like(st_ref)

    cidx = _rwkv_chunk_index(d, pl.program_id(2), nctx, nc)
    is_ctx = cidx < nctx
    first = jnp.logical_or(cidx == 0, cidx == nctx)
    last = jnp.logical_or(cidx == nctx - 1, cidx == nc - 1)
    z = jnp.where(is_ctx, zc_ref[0], zl_ref[0])
    prev_row = jnp.where(is_ctx, zcp_ref[0, SUBLANE - 1:SUBLANE, :], zlp_ref[0, SUBLANE - 1:SUBLANE, :])
    next_row = jnp.where(is_ctx, zcn_ref[0, 0:1, :], zln_ref[0, 0:1, :])
    prev_row = jnp.where(first, 0.0, prev_row)
    next_row = jnp.where(last, 0.0, next_row)
    trow = lax.broadcasted_iota(jnp.int32, (C, 1), 0)
    zp = jnp.where(trow == 0, prev_row, pltpu.roll(z, 1, axis=0))
    zn = jnp.where(trow == C - 1, next_row, pltpu.roll(z, C - 1, axis=0))
    u_ref[0] = z + mu_ref[0:1, :] * (zp - z) + mu_ref[1:2, :] * (zn - z)
    gs_ref[0, 0] = u_ref[0, :, g_off:g_off + 2 * LANE]

    row = lax.broadcasted_iota(jnp.int32, (C, C), 0)
    col = lax.broadcasted_iota(jnp.int32, (C, C), 1)
    rel = (row - col) * (1 - 2 * d)
    strict = (rel > 0)[None]
    incl = (rel >= 0)[None]
    eye = (row == col).astype(F32)[None]
    mm2 = (((1,), (0,)), ((), ()))

    lora_w = u_ref[0, :, 3 * rw:3 * rw + 2 * R_LORA_W]
    lora_a = u_ref[0, :, 3 * rw + 2 * R_LORA_W:3 * rw + 2 * R_LORA_W + 2 * R_LORA_A]
    w_log = w0_ref[0] + _dot3(jnp.tanh(lora_w), wup_ref[0], mm2)
    nx = -w_log
    softplus = jnp.maximum(nx, 0.0) + jnp.log(1.0 + jnp.exp(-jnp.abs(nx)))
    lw = -jnp.exp(-softplus - 0.5)
    a = jax.nn.sigmoid(a0_ref[0] + _dot3(lora_a, aup_ref[0], mm2))

    tri = jnp.where(rel >= 0, 1.0, 0.0).astype(BF16)
    l1 = lw.astype(BF16)
    r1 = lw - l1.astype(F32)
    l2 = r1.astype(BF16)
    l3 = (r1 - l2.astype(F32)).astype(BF16)
    cum = _dg(tri, l1, mm2) + (_dg(tri, l2, mm2) + _dg(tri, l3, mm2))
    g_end = jnp.exp(jnp.sum(lw, axis=0, keepdims=True))
    k = u_ref[0, :, rw:2 * rw]
    gi = jnp.exp(-cum)
    gp_s[...] = jnp.exp(cum - lw)
    gi_s[...] = gi
    a_s[...] = a
    kn_s[...] = k * kk_ref[...]
    kd = k * (1.0 + (a - 1.0) * ka_ref[...])
    kd_s[...] = kd
    rt_s[...] = u_ref[0, :, :rw] * jnp.exp(cum)

    for h in range(nh):
        sl = slice(h * N, (h + 1) * N)
        kk = kn_s[:, sl]
        kk = kk * lax.rsqrt(jnp.maximum(jnp.sum(kk * kk, axis=-1, keepdims=True), 1e-24))
        r_h = u_ref[0, :, h * N:(h + 1) * N]
        v_h = u_ref[0, :, 2 * rw + h * N:2 * rw + (h + 1) * N]
        kd_h = kd_s[:, sl]
        at3[h] = -kk * gp_s[:, sl]
        bt3[h] = kk * a_s[:, sl] * gi_s[:, sl]
        kt3[h] = kd_h * gi_s[:, sl]
        rt3[h] = rt_s[:, sl]
        v3[h] = v_h
        gc3[h] = g_end[:, sl]
        bon_ref[0, 0, :, sl] = jnp.sum(r_h * kd_h * rk_ref[:, sl], axis=-1, keepdims=True) * v_h

    at = at3[...]
    bt = bt3[...]
    kt = kt3[...]
    rt = rt3[...]
    v = v3[...]
    gc = gc3[...]
    m0 = st_ref[...]
    bmm = (((2,), (1,)), ((0,), (0,)))
    bmt = (((2,), (2,)), ((0,), (0,)))
    btm = (((1,), (1,)), ((0,), (0,)))

    ar = jnp.concatenate([at, rt], axis=1)
    bk = jnp.concatenate([bt, kt], axis=1)
    big = _dot1(ar, bk, bmt)
    a_ab = jnp.where(strict, big[:, :C, :C], 0.0)
    a_ak = jnp.where(strict, big[:, :C, C:], 0.0)
    a_rb = jnp.where(incl, big[:, C:, :C], 0.0)
    a_rk = jnp.where(incl, big[:, C:, C:], 0.0)

    tinv = eye + a_ab
    p = _dot3(a_ab, a_ab, bmm)
    for _ in range(int(math.log2(C)) - 2):
        both = _dot3(jnp.concatenate([tinv, p], axis=1), p, bmm)
        tinv = tinv + both[:, :C]
        p = both[:, C:]
    tinv = tinv + _dot3(tinv, p, bmm)

    av = _dot1(jnp.concatenate([a_ak, a_rk], axis=1), v, bmm)
    akv = av[:, :C]
    w12 = _dot3(tinv, jnp.concatenate([at, akv], axis=2), bmm)
    rb12 = _dot1(a_rb, w12, bmm)
    rq = rt + rb12[:, :, :at.shape[2]]
    yc = rb12[:, :, at.shape[2]:] + av[:, C:]
    w1 = w12[:, :, :at.shape[2]]
    w2 = w12[:, :, at.shape[2]:]

    um = _dot1(jnp.concatenate([w1, rq], axis=1), m0, bmt)
    u = um[:, :C] + w2
    y3 = um[:, C:] + yc
    for h in range(nh):
        y_ref[0, 0, :, h * N:(h + 1) * N] = y3[h]
    upd = _dot1(jnp.concatenate([u, v], axis=1), bk, btm)
    st_ref[...] = (m0 + upd) * gc


def rwkv_scan(zc, zl, cols, mu, w0, w_up, a0, a_up, k_k, k_a, r_k):
    B, Lc, _ = zc.shape
    Ll = zl.shape[1]
    T = Lc + Ll
    rw = w0.shape[-1]
    C = RWKV_CHUNK
    N = R_HEAD
    H = rw // N
    nc = T // C
    nctx = Lc // C
    sub = C // SUBLANE
    g_off = 3 * rw + 2 * R_LORA_W + 2 * R_LORA_A

    def chunk(d, c):
        return _rwkv_chunk_index(d, c, nctx, nc)

    def seg(d, c, ctx):
        ci = chunk(d, c)
        return jnp.clip(ci, 0, nctx - 1) if ctx else jnp.clip(ci - nctx, 0, nc - nctx - 1)

    def main(ctx):
        return pl.BlockSpec((1, C, cols), lambda d, b, c: (b, seg(d, c, ctx), 0))

    def halo(ctx, after):
        nrow = (Lc if ctx else Ll) // SUBLANE

        def index(d, b, c):
            first = seg(d, c, ctx) * sub
            return (b, jnp.clip(first + sub if after else first - 1, 0, nrow - 1), 0)
        return pl.BlockSpec((1, SUBLANE, cols), index)

    def widen(w):
        z = jnp.zeros_like(w[0])
        return jnp.stack([jnp.concatenate([w[0], z], axis=0), jnp.concatenate([z, w[1]], axis=0)])

    dir_vec = pl.BlockSpec((1, 1, rw), lambda d, b, c: (d, 0, 0))
    dir_mat = pl.BlockSpec((1, 2 * R_LORA_W, rw), lambda d, b, c: (d, 0, 0))
    vec = pl.BlockSpec((1, rw), lambda d, b, c: (0, 0))
    out_spec = pl.BlockSpec((1, 1, C, rw), lambda d, b, c: (d, b, chunk(d, c), 0))
    gs_spec = pl.BlockSpec((1, 1, C, 2 * LANE), lambda d, b, c: (d, b, chunk(d, c), 0))
    full = pltpu.VMEM((C, rw), F32)
    per_head = pltpu.VMEM((H, C, N), F32)
    mu_pad = jnp.pad(mu, ((0, 0), (0, cols - mu.shape[1])))
    return pl.pallas_call(
        functools.partial(_rwkv_body, rw=rw, nctx=nctx, g_off=g_off),
        grid=(2, B, nc),
        in_specs=[main(True), halo(True, False), halo(True, True),
                  main(False), halo(False, False), halo(False, True),
                  pl.BlockSpec((2, cols), lambda d, b, c: (0, 0)),
                  dir_vec, dir_mat, dir_vec, dir_mat, vec, vec, vec],
        out_specs=[out_spec, out_spec, gs_spec],
        out_shape=[jax.ShapeDtypeStruct((2, B, T, rw), F32), jax.ShapeDtypeStruct((2, B, T, rw), F32),
                   jax.ShapeDtypeStruct((2, B, T, 2 * LANE), F32)],
        scratch_shapes=[pltpu.VMEM((H, N, N), F32), full, full, full, full, full, full,
                        per_head, per_head, per_head, per_head, per_head, pltpu.VMEM((H, 1, N), F32),
                        pltpu.VMEM((1, C, cols), F32)],
        compiler_params=pltpu.CompilerParams(
            dimension_semantics=("parallel", "parallel", "arbitrary"), vmem_limit_bytes=VMEM_LIMIT),
    )(zc, zc, zc, zl, zl, zl, mu_pad, w0[:, None, :], widen(w_up), a0[:, None, :], widen(a_up),
      k_k.reshape(1, rw), k_a.reshape(1, rw), r_k.reshape(1, rw))


def _rwkv_readout_body(y_ref, bon_ref, u_ref, gup_ref, lw_ref, lb_ref, o_ref, *, rw):
    N = R_HEAD
    ys = y_ref[0, 0] + y_ref[1, 0]
    gate = _dot3(jax.nn.sigmoid(u_ref[0, 0]), gup_ref[...], (((1,), (0,)), ((), ())))
    extra = bon_ref[0, 0] + bon_ref[1, 0] + lb_ref[...]
    for h in range(rw // N):
        sl = slice(h * N, (h + 1) * N)
        t = ys[:, sl]
        mean = jnp.mean(t, axis=-1, keepdims=True)
        var = jnp.mean(jnp.square(t - mean), axis=-1, keepdims=True)
        yn = (t - mean) * lax.rsqrt(var + R_GN_EPS) * lw_ref[:, sl]
        o_ref[0, :, sl] = (yn + extra[:, sl]) * gate[:, sl]


def rwkv_readout(y, bonus, gs, g_up, lnx_w, lnx_b):
    _, B, T, rw = y.shape
    tm = _pick(T, (256, 128, 64))
    gw = gs.shape[-1]
    g_pad = jnp.pad(g_up, ((0, gw - g_up.shape[0]), (0, 0)))
    pair = pl.BlockSpec((2, 1, tm, rw), lambda b, i: (0, b, i, 0))
    vec = pl.BlockSpec((1, rw), lambda b, i: (0, 0))
    return pl.pallas_call(
        functools.partial(_rwkv_readout_body, rw=rw),
        grid=(B, T // tm),
        in_specs=[pair, pair, pl.BlockSpec((1, 1, tm, gw), lambda b, i: (0, b, i, 0)),
                  pl.BlockSpec((gw, rw), lambda b, i: (0, 0)), vec, vec],
        out_specs=pl.BlockSpec((1, tm, rw), lambda b, i: (b, i, 0)),
        out_shape=jax.ShapeDtypeStruct((B, T, rw), F32),
        compiler_params=pltpu.CompilerParams(
            dimension_semantics=("parallel", "parallel"), vmem_limit_bytes=VMEM_LIMIT),
    )(y, bonus, gs, g_pad, lnx_w.reshape(1, rw), lnx_b.reshape(1, rw))


def _moe_up_body(be_ref, nu_ref, src_ref, nxt_ref, h_hbm, wg_ref, wu_ref, o_ref,
                 wgs_ref, wus_ref, rows_ref, xs_ref, sem):
    i = pl.program_id(0)
    nblk = pl.num_programs(0)
    slot = lax.rem(i, 2)
    prev = be_ref[jnp.maximum(i - 1, 0)]
    fresh = jnp.logical_or(i == 0, be_ref[i] != prev)

    def row_copy(idx_ref, r, dst_slot):
        return pltpu.make_async_copy(h_hbm.at[pl.ds(idx_ref[0, 0, r], 1), :],
                                     rows_ref.at[dst_slot, pl.ds(r, 1), :], sem.at[dst_slot])

    def request(idx_ref, dst_slot):
        for r in range(MOE_BLOCK):
            row_copy(idx_ref, r, dst_slot).start()

    def drain(dst_slot):
        for r in range(MOE_BLOCK):
            pltpu.make_async_copy(h_hbm.at[pl.ds(0, 1), :], rows_ref.at[dst_slot, pl.ds(r, 1), :],
                                  sem.at[dst_slot]).wait()

    @pl.when(i == 0)
    def _():
        request(src_ref, 0)

    @pl.when(fresh)
    def _():
        wgs_ref[...] = wg_ref[0, 0].astype(BF16)
        wus_ref[...] = wu_ref[0, 0].astype(BF16)

    drain(slot)
    xs_ref[...] = rows_ref[slot].astype(BF16)
    request(nxt_ref, 1 - slot)

    @pl.when(i < nu_ref[0])
    def _():
        x = xs_ref[...]
        g = jnp.dot(x, wgs_ref[...], preferred_element_type=F32)
        u = jnp.dot(x, wus_ref[...], preferred_element_type=F32)
        o_ref[...] = (g * jax.nn.sigmoid(g) * u).astype(o_ref.dtype)

    @pl.when(i >= nu_ref[0])
    def _():
        o_ref[...] = jnp.zeros_like(o_ref)

    @pl.when(i == nblk - 1)
    def _():
        drain(1 - slot)


def _moe_down_body(be_ref, nu_ref, h_ref, wd_ref, o_ref, wds_ref):
    i = pl.program_id(0)
    prev = be_ref[jnp.maximum(i - 1, 0)]
    fresh = jnp.logical_or(i == 0, be_ref[i] != prev)

    @pl.when(fresh)
    def _():
        wds_ref[...] = wd_ref[0, 0].astype(BF16)

    @pl.when(i < nu_ref[0])
    def _():
        o_ref[...] = jnp.dot(h_ref[...], wds_ref[...], preferred_element_type=F32).astype(o_ref.dtype)

    @pl.when(i >= nu_ref[0])
    def _():
        o_ref[...] = jnp.zeros_like(o_ref)


def moe_experts(h, src, block_expert, n_used, w_gate, w_up, w_down, layer):
    D = h.shape[1]
    R = src.shape[0]
    Hd = w_gate.shape[-1]
    nblk = R // MOE_BLOCK
    idx = src.reshape(nblk, 1, MOE_BLOCK)
    hid = pl.pallas_call(
        _moe_up_body,
        grid_spec=pltpu.PrefetchScalarGridSpec(
            num_scalar_prefetch=2,
            grid=(nblk,),
            in_specs=[pl.BlockSpec((1, 1, MOE_BLOCK), lambda i, be, nu: (i, 0, 0), memory_space=pltpu.SMEM),
                      pl.BlockSpec((1, 1, MOE_BLOCK), lambda i, be, nu: (jnp.minimum(i + 1, nblk - 1), 0, 0),
                                   memory_space=pltpu.SMEM),
                      pl.BlockSpec(memory_space=pl.ANY),
                      pl.BlockSpec((1, 1, D, Hd), lambda i, be, nu: (layer, be[i], 0, 0)),
                      pl.BlockSpec((1, 1, D, Hd), lambda i, be, nu: (layer, be[i], 0, 0))],
            out_specs=pl.BlockSpec((MOE_BLOCK, Hd), lambda i, be, nu: (i, 0)),
            scratch_shapes=[pltpu.VMEM((D, Hd), BF16), pltpu.VMEM((D, Hd), BF16),
                            pltpu.VMEM((2, MOE_BLOCK, D), F32), pltpu.VMEM((MOE_BLOCK, D), BF16),
                            pltpu.SemaphoreType.DMA((2,))]),
        out_shape=jax.ShapeDtypeStruct((R, Hd), BF16),
        compiler_params=pltpu.CompilerParams(
            dimension_semantics=("arbitrary",), vmem_limit_bytes=VMEM_LIMIT),
    )(block_expert, n_used, idx, idx, h, w_gate, w_up)
    return pl.pallas_call(
        _moe_down_body,
        grid_spec=pltpu.PrefetchScalarGridSpec(
            num_scalar_prefetch=2,
            grid=(nblk,),
            in_specs=[pl.BlockSpec((MOE_BLOCK, Hd), lambda i, be, nu: (i, 0)),
                      pl.BlockSpec((1, 1, Hd, D), lambda i, be, nu: (layer, be[i], 0, 0))],
            out_specs=pl.BlockSpec((MOE_BLOCK, D), lambda i, be, nu: (i, 0)),
            scratch_shapes=[pltpu.VMEM((Hd, D), BF16)]),
        out_shape=jax.ShapeDtypeStruct((R, D), F32),
        compiler_params=pltpu.CompilerParams(
            dimension_semantics=("arbitrary",), vmem_limit_bytes=VMEM_LIMIT),
    )(block_expert, n_used, hid, w_down)


def moe_layer(h, logits, w_gate, w_up, w_down, layer):
    T, D = h.shape
    lg = logits[:, :N_GROUPS]
    g_idx = jnp.argmax(lg, axis=-1)
    p_group = jnp.take_along_axis(jax.nn.softmax(lg, axis=-1), g_idx[:, None], axis=-1)
    le = logits[:, N_GROUPS:N_GROUPS + N_EXPERTS].reshape(T, N_GROUPS, EXPERTS_PER_GROUP)
    le_sel = jnp.take_along_axis(le, g_idx[:, None, None], axis=1)[:, 0]
    top_v, top_i = lax.top_k(le_sel, TOP_K)
    gate = p_group * jax.nn.softmax(top_v, axis=-1)
    expert = (g_idx[:, None] * EXPERTS_PER_GROUP + top_i).astype(jnp.int32)

    A = T * TOP_K
    e_flat = expert.reshape(A)
    onehot = (e_flat[:, None] == jnp.arange(N_EXPERTS, dtype=jnp.int32)[None, :]).astype(jnp.int32)
    rank = jnp.take_along_axis(jnp.cumsum(onehot, axis=0), e_flat[:, None], axis=1)[:, 0] - 1
    counts = jnp.sum(onehot, axis=0)
    padded = (counts + MOE_BLOCK - 1) // MOE_BLOCK * MOE_BLOCK
    pad_end = jnp.cumsum(padded)
    pad_start = pad_end - padded
    dest = pad_start[e_flat] + rank
    nblk = -(-A // MOE_BLOCK) + N_EXPERTS
    R = nblk * MOE_BLOCK
    n_used = (pad_end[-1] // MOE_BLOCK).astype(jnp.int32)
    blk = jnp.arange(nblk, dtype=jnp.int32)
    be = jnp.sum((pad_end[None, :] <= (blk * MOE_BLOCK)[:, None]).astype(jnp.int32), axis=1)
    be = jnp.minimum(be, N_EXPERTS - 1)
    be = jnp.where(blk < n_used, be, be[jnp.maximum(n_used - 1, 0)])
    src = jnp.zeros((R,), jnp.int32).at[dest].set(jnp.arange(A, dtype=jnp.int32) // TOP_K)
    ys = moe_experts(h, src, be, n_used.reshape(1), w_gate, w_up, w_down, layer)
    d2 = dest.reshape(T, TOP_K)
    return (gate[:, 0:1] * jnp.take(ys, d2[:, 0], axis=0, mode='clip')
            + gate[:, 1:2] * jnp.take(ys, d2[:, 1], axis=0, mode='clip'))


def _rope_tables(L):
    half = A_HEAD_DIM // 2
    inv_freq = ROPE_BASE ** (-jnp.arange(0, half, 2, dtype=F32) / half)
    t = jnp.arange(L)
    row = (t // GRID_W).astype(F32)
    col = (t % GRID_W).astype(F32)
    ar = row[:, None] * inv_freq[None, :]
    ac = col[:, None] * inv_freq[None, :]
    cos = jnp.concatenate([jnp.cos(ar), jnp.cos(ar), jnp.cos(ac), jnp.cos(ac)], axis=-1)
    sin = jnp.concatenate([-jnp.sin(ar), jnp.sin(ar), -jnp.sin(ac), jnp.sin(ac)], axis=-1)
    reps = LANE // A_HEAD_DIM
    return jnp.tile(cos, (1, reps)), jnp.tile(sin, (1, reps))


def _rwkv_mixer(zc, zl, cols, mu, w0, w_up, a0, a_up, g_up, k_k, k_a, r_k, lnx_w, lnx_b):
    y, bonus, gs = rwkv_scan(zc, zl, cols, mu, w0, w_up, a0, a_up, k_k, k_a, r_k)
    return rwkv_readout(y, bonus, gs, g_up, lnx_w, lnx_b)


def _dft_table(L):
    N = 2 * L
    half = DFT_TILE // 2
    k = jnp.arange(L, dtype=jnp.int32)
    n = jnp.arange(L, dtype=jnp.int32)
    m = ((2 * k + 1)[:, None] * n[None, :]) % (2 * N)
    ang = m.astype(F32) * (math.pi / N)
    tab = jnp.stack([jnp.cos(ang).reshape(L // half, half, L), (-jnp.sin(ang)).reshape(L // half, half, L)], axis=1)
    return tab.reshape(2 * L, L)


def _hyena_filters(L, w1, b1, f1, w2, b2, f2, w3, width):
    hp = lax.Precision.HIGHEST
    t = jnp.linspace(0.0, 1.0, L, dtype=F32)[:, None]
    bands = jnp.linspace(1e-4, HY_BANDS - 1, HY_BANDS, dtype=F32)[None, :]
    ang = (2 * math.pi) * jnp.arange(L, dtype=F32)[:, None] / L * bands
    z = jnp.concatenate([t, jnp.cos(ang), -jnp.sin(ang)], axis=-1)
    h = jnp.sin(f1 * (jnp.dot(z, w1, precision=hp) + b1))
    h = jnp.sin(f2 * (jnp.dot(h, w2, precision=hp) + b2))
    h = jnp.dot(h, w3, precision=hp)
    deltas = jnp.abs(jnp.linspace(HY_MIN_DECAY, HY_MAX_DECAY, width, dtype=F32))
    h = h * jnp.exp(-t * jnp.tile(deltas, 2)[None, :])
    row0 = (jnp.arange(L) > 0).astype(F32)[:, None]
    h_bwd = h[:, width:] * row0
    h = jnp.concatenate([h[:, :width], h_bwd], axis=1)
    norm = jnp.sum(jnp.abs(h[:, :width]), axis=0, keepdims=True) + jnp.sum(jnp.abs(h_bwd), axis=0, keepdims=True)
    return h, norm


def _hy_pre_body(z_ref, zp_ref, zn_ref, w_ref, b_ref, hb_ref, u_ref, t_ref, x0_ref, *, width):
    i = pl.program_id(1)
    last = pl.num_programs(1) - 1
    z = z_ref[0]
    tl = z.shape[0]
    prev_row = jnp.where(i > 0, zp_ref[0, SUBLANE - 1:SUBLANE, :], 0.0)
    next_row = jnp.where(i < last, zn_ref[0, 0:1, :], 0.0)
    row = lax.broadcasted_iota(jnp.int32, (tl, 1), 0)
    up = jnp.where(row == 0, prev_row, pltpu.roll(z, 1, axis=0))
    dn = jnp.where(row == tl - 1, next_row, pltpu.roll(z, tl - 1, axis=0))
    c = up * w_ref[0:1, :] + z * w_ref[1:2, :] + dn * w_ref[2:3, :] + b_ref[...]
    u = c[:, 2 * width:] * c[:, width:2 * width]
    u_ref[0] = u.astype(u_ref.dtype)
    t_ref[0] = u * hb_ref[...]
    x0_ref[0] = c[:, :width]


def hyena_pre(z, conv_w, conv_b, bias):
    B, L, W3 = z.shape
    W = W3 // 3
    tl = _pick(L, (256, 128, 64, 32, 16, 8))
    nsub = tl // SUBLANE
    out = pl.BlockSpec((1, tl, W), lambda b, i: (b, i, 0))
    return pl.pallas_call(
        functools.partial(_hy_pre_body, width=W),
        grid=(B, L // tl),
        in_specs=[pl.BlockSpec((1, tl, W3), lambda b, i: (b, i, 0)),
                  pl.BlockSpec((1, SUBLANE, W3), lambda b, i: (b, jnp.maximum(i * nsub - 1, 0), 0)),
                  pl.BlockSpec((1, SUBLANE, W3), lambda b, i: (b, jnp.minimum((i + 1) * nsub, L // SUBLANE - 1), 0)),
                  pl.BlockSpec((3, W3), lambda b, i: (0, 0)),
                  pl.BlockSpec((1, W3), lambda b, i: (0, 0)),
                  pl.BlockSpec((1, W), lambda b, i: (0, 0))],
        out_specs=[out, out, out],
        out_shape=[jax.ShapeDtypeStruct((B, L, W), BF16), jax.ShapeDtypeStruct((B, L, W), F32),
                   jax.ShapeDtypeStruct((B, L, W), F32)],
        compiler_params=pltpu.CompilerParams(
            dimension_semantics=("parallel", "parallel"), vmem_limit_bytes=VMEM_LIMIT),
    )(z, z, z, conv_w, conv_b.reshape(1, W3), bias.reshape(1, W))


def _hyena_core(z, conv_w, conv_b, filt, bias):
    B, L, W3 = z.shape
    W = W3 // 3
    h, norm = filt
    u, t, x0 = hyena_pre(z, conv_w, conv_b, bias)
    tab = _dft_table(L)
    fwd = tab.astype(BF16)
    inv = tab.T.astype(BF16)
    hf = mm(fwd, h)
    half = DFT_TILE // 2
    sign = jnp.tile(jnp.concatenate([jnp.ones((half, 1), F32), -jnp.ones((half, 1), F32)]), (2 * L // DFT_TILE, 1))
    kf = (hf[:, :W] + sign * hf[:, W:]) / norm
    outs = []
    for b in range(B):
        yf = mm(fwd, u[b], cmul=kf, out_dtype=BF16)
        outs.append(mm(inv, yf, post=(t[b], x0[b], 1.0 / L)))
    return jnp.stack(outs)


def kernel(x, c, ctx, c_ctx, mod_w, mod_b, norm1_w, norm2_w, router_group_w, router_expert_w, expert_w_gate,
           expert_w_up, expert_w_down, mix_w_in, mix_w_out, attn_sink, rwkv_shift_mu, rwkv_w0, rwkv_w_up,
           rwkv_a0, rwkv_a_up, rwkv_g_up, rwkv_k_k, rwkv_k_a, rwkv_r_k, rwkv_lnx_w, rwkv_lnx_b, hy_w_in,
           hy_b_in, hy_conv_w, hy_conv_b, hy_ffn_w1, hy_ffn_b1, hy_sin_f1, hy_ffn_w2, hy_ffn_b2, hy_sin_f2,
           hy_ffn_w3, hy_bias, hy_w_out, final_norm_w):
    B, L, D = x.shape
    Cn = ctx.shape[1]
    depth = mod_w.shape[0]
    a_q = D // 2
    a_kv = a_q // A_HEAD_DIM // A_GROUP * A_HEAD_DIM
    a_cols = a_q + 2 * a_kv
    rw = D // 2
    in_cols = mix_w_in.shape[-1]
    r_cols = -(-(in_cols - a_cols) // (2 * LANE)) * (2 * LANE)
    q_off = -(-r_cols // a_q) * a_q
    in_pad = q_off + a_cols
    assert in_pad % (2 * LANE) == 0

    cc = jnp.concatenate([c, c_ctx[None], jnp.zeros((SUBLANE - B - 1, D), F32)], axis=0)
    sc = jax.nn.silu(cc)
    mods = [mm(sc, mod_w, w_idx=i, bias=mod_b[i]).reshape(SUBLANE, 6, D) for i in range(depth)]
    cos, sin = _rope_tables(L)
    rope = (cos, sin, cos, sin)

    xl = x.reshape(B * L, D)
    xc = ctx.reshape(B * Cn, D)
    for i in range(depth):
        ctx_later = any(j % 2 == 0 for j in range(i + 1, depth))
        ctx_here = ctx_later or (i % 2 == 0)
        ml_ = mods[i][:B]
        mc_ = jnp.broadcast_to(mods[i][B:B + 1], (B, 6, D))

        def nrm(m, gain, a, b):
            return ((gain * (1 + m[:, b]))[:, None, :], m[:, a][:, None, :])

        n1l = nrm(ml_, norm1_w[i], 0, 1)
        n1c = nrm(mc_, norm1_w[i], 0, 1)
        gate_l = ml_[:, 2][:, None, :]
        gate_c = mc_[:, 2][:, None, :]
        if i % 2 == 0:
            e = i // 2
            wm = mix_w_in[e]
            w_in = jnp.concatenate([wm[:, a_cols:], jnp.zeros((D, q_off - (in_cols - a_cols)), F32),
                                    wm[:, :a_cols]], axis=1).astype(BF16)
            zl = mm(xl, w_in, norm=n1l, rows_per_group=L, tn=in_pad // 2).reshape(B, L, in_pad)
            zc = mm(xc, w_in, norm=n1c, rows_per_group=Cn, tn=in_pad // 2).reshape(B, Cn, in_pad)
            att_l = attention(zl, zc, attn_sink[e], rope, a_q=a_q, a_kv=a_kv, q_off=q_off, band=True)
            rw_all = _rwkv_mixer(zc, zl, r_cols, rwkv_shift_mu[e], rwkv_w0[e], rwkv_w_up[e], rwkv_a0[e],
                                 rwkv_a_up[e], rwkv_g_up[e], rwkv_k_k[e], rwkv_k_a[e], rwkv_r_k[e],
                                 rwkv_lnx_w[e], rwkv_lnx_b[e])
            w_out = mix_w_out[e].astype(BF16)
            xl = mm(att_l.reshape(B * L, a_q), w_out, x2=rw_all[:, Cn:].reshape(B * L, rw),
                    res=(xl, gate_l), rows_per_group=L)
            if ctx_later:
                att_c = attention(zl, zc, attn_sink[e], None, a_q=a_q, a_kv=a_kv, q_off=q_off, band=False)
                xc = mm(att_c.reshape(B * Cn, a_q), w_out, x2=rw_all[:, :Cn].reshape(B * Cn, rw),
                        res=(xc, gate_c), rows_per_group=Cn)
        else:
            o = i // 2
            w_in = hy_w_in[o].astype(BF16)
            w_out = hy_w_out[o].astype(BF16)
            fl = (hy_ffn_w1[o], hy_ffn_b1[o], hy_sin_f1[o], hy_ffn_w2[o], hy_ffn_b2[o], hy_sin_f2[o], hy_ffn_w3[o])
            zl = mm(xl, w_in, norm=n1l, bias=hy_b_in[o], rows_per_group=L).reshape(B, L, 3 * D)
            gl = _hyena_core(zl, hy_conv_w[o], hy_conv_b[o], _hyena_filters(L, *fl, D), hy_bias[o])
            xl = mm(gl.reshape(B * L, D), w_out, res=(xl, gate_l), rows_per_group=L)
            if ctx_later:
                zc = mm(xc, w_in, norm=n1c, bias=hy_b_in[o], rows_per_group=Cn).reshape(B, Cn, 3 * D)
                gc = _hyena_core(zc, hy_conv_w[o], hy_conv_b[o], _hyena_filters(Cn, *fl, D), hy_bias[o])
                xc = mm(gc.reshape(B * Cn, D), w_out, res=(xc, gate_c), rows_per_group=Cn)

        w_route = jnp.concatenate([router_group_w[i], router_expert_w[i]], axis=1)
        w_route = jnp.pad(w_route, ((0, 0), (0, LANE - w_route.shape[1])))
        n2l = nrm(ml_, norm2_w[i], 3, 4)
        lg_l, h_l = mm(xl, w_route, norm=n2l, rows_per_group=L, x3=True, emit_h=F32)
        if ctx_later:
            n2c = nrm(mc_, norm2_w[i], 3, 4)
            lg_c, h_c = mm(xc, w_route, norm=n2c, rows_per_group=Cn, x3=True, emit_h=F32)
            f = moe_layer(jnp.concatenate([h_l, h_c], axis=0), jnp.concatenate([lg_l, lg_c], axis=0),
                          expert_w_gate, expert_w_up, expert_w_down, i)
            xl = xl + jnp.repeat(ml_[:, 5], L, axis=0) * f[:B * L]
            xc = xc + jnp.repeat(mc_[:, 5], Cn, axis=0) * f[B * L:]
        else:
            f = moe_layer(h_l, lg_l, expert_w_gate, expert_w_up, expert_w_down, i)
            xl = xl + jnp.repeat(ml_[:, 5], L, axis=0) * f

    xf = xl.reshape(B, L, D)
    y = xf * lax.rsqrt(jnp.mean(xf * xf, axis=-1, keepdims=True) + NORM_EPS)
    return y * final_norm_w
```

```python
import functools
import math

import jax
import jax.numpy as jnp
from jax import lax
from jax.experimental import pallas as pl
from jax.experimental.pallas import tpu as pltpu

F32 = jnp.float32
BF16 = jnp.bfloat16

GRID_W = 64
NORM_EPS = 1e-6
A_HEAD_DIM = 64
A_GROUP = 8
A_WINDOW = 128
A_BLOCK = 128
ROPE_BASE = 10000.0
NEG_INF = -1e30
R_HEAD = 64
R_LORA_W = 64
R_LORA_A = 64
R_GN_EPS = 64e-5
HY_BANDS = 16
HY_MIN_DECAY = math.log(1e-2) / 1.5
HY_MAX_DECAY = math.log(1e-2) / 0.3
N_GROUPS = 4
EXPERTS_PER_GROUP = 8
N_EXPERTS = N_GROUPS * EXPERTS_PER_GROUP
TOP_K = 2

LANE = 128
SUBLANE = 8
VMEM_LIMIT = 56 * 1024 * 1024
MM_TILE_BYTES = 8 * 1024 * 1024
DFT_TILE = 512
RWKV_CHUNK = 64
RWKV_SEQS = 2
MOE_BLOCK = 256


def _pick(n, cands):
    for c in cands:
        if n % c == 0:
            return c
    return n


def _split_bf16(x):
    hi = x.astype(BF16)
    lo = (x - hi.astype(F32)).astype(BF16)
    return hi, lo


def _dg(a, b, dims):
    return lax.dot_general(a, b, dims, preferred_element_type=F32)


def _dot3(a, b, dims):
    ah, al = _split_bf16(a)
    bh, bl = _split_bf16(b)
    ax = a.ndim - 2
    m = a.shape[ax]
    if dims[0][0][0] == ax:
        return _dg(ah, bh, dims) + (_dg(ah, bl, dims) + _dg(al, bh, dims))
    both = _dg(jnp.concatenate([ah, al], axis=ax), bh, dims)
    return (lax.slice_in_dim(both, 0, m, axis=ax) + lax.slice_in_dim(both, m, 2 * m, axis=ax)) + _dg(ah, bl, dims)


def _dot1(a, b, dims):
    return _dg(a.astype(BF16), b.astype(BF16), dims)


def _mm_body(*refs, norm, bias, res, x3, emit_h, w3d, two_x, cmul, post, batched):
    it = iter(refs)
    x_ref = next(it)
    x2_ref = next(it) if two_x else None
    w_ref = next(it)
    g_ref = next(it) if norm else None
    s_ref = next(it) if norm else None
    b_ref = next(it) if bias else None
    r_ref = next(it) if res else None
    gt_ref = next(it) if res else None
    k_ref = next(it) if cmul else None
    pt_ref = next(it) if post else None
    px_ref = next(it) if post else None
    o_ref = next(it)
    h_ref = next(it) if emit_h else None
    xh_ref = next(it)
    xl_ref = next(it) if x3 else None

    @pl.when(pl.program_id(2 if batched else 1) == 0)
    def _():
        x = x_ref[...].astype(F32)
        if norm:
            ms = jnp.mean(x * x, axis=-1, keepdims=True)
            x = x * lax.rsqrt(ms + NORM_EPS) * g_ref[0] + s_ref[0]
        xh = x.astype(BF16)
        if two_x:
            xh_ref[:, :x.shape[1]] = xh
            xh_ref[:, x.shape[1]:] = x2_ref[...].astype(BF16)
        else:
            xh_ref[...] = xh
        if x3:
            xl_ref[...] = (x - xh.astype(F32)).astype(BF16)
        if emit_h:
            h_ref[...] = x.astype(h_ref.dtype)

    w = w_ref[0] if (w3d or batched) else w_ref[...]
    wh = w.astype(BF16)
    acc = jnp.dot(xh_ref[...], wh, preferred_element_type=F32)
    if x3:
        wl = (w.astype(F32) - wh.astype(F32)).astype(BF16)
        acc = acc + (jnp.dot(xh_ref[...], wl, preferred_element_type=F32)
                     + jnp.dot(xl_ref[...], wh, preferred_element_type=F32))
    if bias:
        acc = acc + b_ref[...]
    if res:
        acc = r_ref[...] + gt_ref[0] * acc
    if cmul:
        h = acc.shape[0] // 2
        re, im = acc[:h], acc[h:]
        kr, ki = k_ref[:h, :], k_ref[h:, :]
        acc = jnp.concatenate([re * kr - im * ki, re * ki + im * kr], axis=0)
    if post:
        acc = (acc * post + (pt_ref[0] if batched else pt_ref[...])) * (px_ref[0] if batched else px_ref[...])
    if batched:
        o_ref[0] = acc.astype(o_ref.dtype)
    else:
        o_ref[...] = acc.astype(o_ref.dtype)


def _mm_tiles(M, K, N, rows_per_group, w_itemsize):
    tm = _pick(math.gcd(M, rows_per_group), (512, 256, 128, 64, 32, 16, 8))
    cands = [t for t in (2048, 1536, 1024, 768, 640, 512, 384, 256, 128)
             if K * t * w_itemsize <= MM_TILE_BYTES and tm * t * 4 <= MM_TILE_BYTES]
    tn = N if (N <= cands[0] and N % LANE == 0) else _pick(N, cands)
    return tm, tn


def mm(x, w, *, x2=None, w_idx=None, norm=None, bias=None, res=None, rows_per_group=None, x3=False,
       emit_h=None, out_dtype=F32, tn=None, cmul=None, post=None, nb=None):
    M, K1 = x.shape
    K = K1 if x2 is None else K1 + x2.shape[1]
    assert x2 is None or (norm is None and not x3 and emit_h is None)
    N = w.shape[-1]
    if rows_per_group is None:
        rows_per_group = M
    tm, tn_auto = _mm_tiles(M, K, N, rows_per_group, w.dtype.itemsize)
    tn = tn_auto if tn is None else tn
    assert M % tm == 0 and N % tn == 0 and rows_per_group % tm == 0
    bpg = rows_per_group // tm
    assert nb is None or (w_idx is None and x2 is None and norm is None and bias is None and res is None
                          and emit_h is None)

    def shared(shape, f):
        return pl.BlockSpec(shape, f if nb is None else (lambda b, i, j: f(i, j)))

    def per_problem(shape, f):
        if nb is None:
            return pl.BlockSpec(shape, f)
        return pl.BlockSpec((1,) + shape, lambda b, i, j: (b,) + f(i, j))

    if w_idx is None:
        w_spec = per_problem((K, tn), lambda i, j: (0, j))
    else:
        w_spec = pl.BlockSpec((1, K, tn), lambda i, j: (w_idx, 0, j))
    in_specs = [shared((tm, K1), lambda i, j: (i, 0))]
    args = [x]
    if x2 is not None:
        in_specs.append(pl.BlockSpec((tm, K - K1), lambda i, j: (i, 0)))
        args.append(x2)
    in_specs.append(w_spec)
    args.append(w)
    if norm is not None:
        in_specs += [pl.BlockSpec((1, 1, K), lambda i, j: (i // bpg, 0, 0))] * 2
        args += [norm[0], norm[1]]
    if bias is not None:
        in_specs.append(pl.BlockSpec((1, tn), lambda i, j: (0, j)))
        args.append(bias.reshape(1, N).astype(F32))
    if res is not None:
        in_specs.append(pl.BlockSpec((tm, tn), lambda i, j: (i, j)))
        in_specs.append(pl.BlockSpec((1, 1, tn), lambda i, j: (i // bpg, 0, j)))
        args += [res[0], res[1]]
    if cmul is not None:
        in_specs.append(shared((tm, tn), lambda i, j: (i, j)))
        args.append(cmul)
    if post is not None:
        in_specs += [per_problem((tm, tn), lambda i, j: (i, j))] * 2
        args += [post[0], post[1]]
    out_shape = [jax.ShapeDtypeStruct((M, N) if nb is None else (nb, M, N), out_dtype)]
    out_specs = [per_problem((tm, tn), lambda i, j: (i, j))]
    if emit_h is not None:
        out_shape.append(jax.ShapeDtypeStruct((M, K), emit_h))
        out_specs.append(pl.BlockSpec((tm, K), lambda i, j: (i, 0)))
    scratch = [pltpu.VMEM((tm, K), BF16)]
    if x3:
        scratch.append(pltpu.VMEM((tm, K), BF16))
    body = functools.partial(_mm_body, norm=norm is not None, bias=bias is not None,
                             res=res is not None, x3=x3, emit_h=emit_h is not None,
                             w3d=w_idx is not None, two_x=x2 is not None, cmul=cmul is not None,
                             post=None if post is None else post[2], batched=nb is not None)
    grid = (M // tm, N // tn)
    sem = ("parallel", "arbitrary")
    outs = pl.pallas_call(
        body,
        grid=grid if nb is None else (nb,) + grid,
        in_specs=in_specs,
        out_specs=out_specs,
        out_shape=out_shape,
        scratch_shapes=scratch,
        compiler_params=pltpu.CompilerParams(
            dimension_semantics=sem if nb is None else ("parallel",) + sem, vmem_limit_bytes=VMEM_LIMIT),
    )(*args)
    return outs if emit_h is not None else outs[0]


def _swap16(x):
    q4 = A_HEAD_DIM // 4
    lane = lax.broadcasted_iota(jnp.int32, x.shape, 1)
    return jnp.where((lane & q4) == 0, pltpu.roll(x, LANE - q4, axis=1), pltpu.roll(x, q4, axis=1))


def _rope_tile(x, cos, sin):
    parts = []
    for j in range(x.shape[1] // LANE):
        t = x[:, j * LANE:(j + 1) * LANE]
        parts.append(t * cos + _swap16(t) * sin)
    return parts[0] if len(parts) == 1 else jnp.concatenate(parts, axis=1)


def _attn_body(*refs, seq, n_kv, band):
    dh = A_HEAD_DIM
    if band:
        (sink_ref, q_ref, k_ref, v_ref, kc_ref, vc_ref, cq_ref, sq_ref, ck_ref, sk_ref,
         o_ref, qs_ref, ks_ref, vs_ref, kcs_ref, vcs_ref) = refs
    else:
        sink_ref, q_ref, kc_ref, vc_ref, o_ref, qs_ref, kcs_ref, vcs_ref = refs
    n = pl.program_id(1)
    scale = dh ** -0.5
    kcs_ref[...] = kc_ref[0].astype(BF16)
    vcs_ref[...] = vc_ref[0].astype(BF16)
    if band:
        win = 3 * A_BLOCK
        start = jnp.clip((n - 1) * A_BLOCK, 0, seq - win)
        start = pl.multiple_of(start, A_BLOCK)
        qpos = n * A_BLOCK + lax.broadcasted_iota(jnp.int32, (A_BLOCK, win), 0)
        kpos = start + lax.broadcasted_iota(jnp.int32, (A_BLOCK, win), 1)
        mask = jnp.abs(kpos - qpos) <= A_WINDOW
        qs_ref[...] = (_rope_tile(q_ref[0], cq_ref[...], sq_ref[...]) * scale).astype(BF16)
        ks_ref[...] = _rope_tile(k_ref[0, pl.ds(start, win), :], ck_ref[pl.ds(start, win), :],
                                 sk_ref[pl.ds(start, win), :]).astype(BF16)
        vs_ref[...] = v_ref[0, pl.ds(start, win), :].astype(BF16)
    else:
        qs_ref[...] = (q_ref[0] * scale).astype(BF16)
    nt = (((1,), (1,)), ((), ()))
    for h in range(n_kv):
        hs = slice(h * dh, (h + 1) * dh)
        kc = kcs_ref[:, hs]
        vc = vcs_ref[:, hs]
        if band:
            kw = ks_ref[:, hs]
            vw = vs_ref[:, hs]
        for g in range(A_GROUP):
            hd = h * A_GROUP + g
            q = qs_ref[:, hd * dh:(hd + 1) * dh]
            sink = sink_ref[hd]
            s_c = _dg(q, kc, nt)
            m = jnp.maximum(jnp.max(s_c, axis=-1, keepdims=True), sink)
            if band:
                s_w = jnp.where(mask, _dg(q, kw, nt), NEG_INF)
                m = jnp.maximum(m, jnp.max(s_w, axis=-1, keepdims=True))
                p_w = jnp.exp(s_w - m)
            p_c = jnp.exp(s_c - m)
            den = jnp.sum(p_c, axis=-1, keepdims=True) + jnp.exp(sink - m)
            o = jnp.dot(p_c.astype(BF16), vc, preferred_element_type=F32)
            if band:
                den = den + jnp.sum(p_w, axis=-1, keepdims=True)
                o = o + jnp.dot(p_w.astype(BF16), vw, preferred_element_type=F32)
            o_ref[0, :, hd * dh:(hd + 1) * dh] = o / den


def attention(z, zc, sink, rope, *, a_q, a_kv, q_off, band):
    src = z if band else zc
    B, L, _ = src.shape
    Cn = zc.shape[1]
    n_kv = a_kv // A_HEAD_DIM
    nb = L // A_BLOCK
    qblk = q_off // a_q
    kblk = (q_off + a_q) // a_kv
    body = functools.partial(_attn_body, seq=L, n_kv=n_kv, band=band)
    q_spec = pl.BlockSpec((1, A_BLOCK, a_q), lambda b, n: (b, n, qblk))
    kc_spec = pl.BlockSpec((1, Cn, a_kv), lambda b, n: (b, 0, kblk))
    vc_spec = pl.BlockSpec((1, Cn, a_kv), lambda b, n: (b, 0, kblk + 1))
    smem = pl.BlockSpec(memory_space=pltpu.SMEM)
    if band:
        cq, sq, ck, sk = rope
        tab_q = pl.BlockSpec((A_BLOCK, LANE), lambda b, n: (n, 0))
        tab_k = pl.BlockSpec((L, LANE), lambda b, n: (0, 0))
        in_specs = [smem, q_spec,
                    pl.BlockSpec((1, L, a_kv), lambda b, n: (b, 0, kblk)),
                    pl.BlockSpec((1, L, a_kv), lambda b, n: (b, 0, kblk + 1)),
                    kc_spec, vc_spec, tab_q, tab_q, tab_k, tab_k]
        args = (sink.astype(F32), z, z, z, zc, zc, cq, sq, ck, sk)
        scratch = [pltpu.VMEM((A_BLOCK, a_q), BF16), pltpu.VMEM((3 * A_BLOCK, a_kv), BF16),
                   pltpu.VMEM((3 * A_BLOCK, a_kv), BF16), pltpu.VMEM((Cn, a_kv), BF16),
                   pltpu.VMEM((Cn, a_kv), BF16)]
    else:
        in_specs = [smem, q_spec, kc_spec, vc_spec]
        args = (sink.astype(F32), zc, zc, zc)
        scratch = [pltpu.VMEM((A_BLOCK, a_q), BF16), pltpu.VMEM((Cn, a_kv), BF16), pltpu.VMEM((Cn, a_kv), BF16)]
    return pl.pallas_call(
        body,
        grid=(B, nb),
        in_specs=in_specs,
        out_specs=pl.BlockSpec((1, A_BLOCK, a_q), lambda b, n: (b, n, 0)),
        out_shape=jax.ShapeDtypeStruct((B, L, a_q), F32),
        scratch_shapes=scratch,
        compiler_params=pltpu.CompilerParams(
            dimension_semantics=("parallel", "arbitrary"), vmem_limit_bytes=VMEM_LIMIT),
    )(*args)


def _rwkv_chunk_index(d, c, nctx, nc):
    back = jnp.where(c < nctx, nctx - 1 - c, nc - 1 - c + nctx)
    return c + d * (back - c)


def _rwkv_body(zc_ref, zcp_ref, zcn_ref, zl_ref, zlp_ref, zln_ref, mu_ref, w0_ref, wup_ref, a0_ref, aup_ref,
               kk_ref, ka_ref, rk_ref, y_ref, bon_ref, gs_ref,
               st_ref, gp_s, gi_s, a_s, kn_s, kd_s, rt_s, at3, bt3, kt3, rt3, v3, gc3, u_ref, *, rw, nctx, g_off):
    C = RWKV_CHUNK
    N = R_HEAD
    nh = rw // N
    nseq = zc_ref.shape[0]
    d = pl.program_id(0)
    nc = pl.num_programs(2)

    @pl.when(pl.program_id(2) == 0)
    def _():
        st_ref[...] = jnp.zeros_like(st_ref)

    cidx = _rwkv_chunk_index(d, pl.program_id(2), nctx, nc)
    is_ctx = cidx < nctx
    first = jnp.logical_or(cidx == 0, cidx == nctx)
    last = jnp.logical_or(cidx == nctx - 1, cidx == nc - 1)
    trow = lax.broadcasted_iota(jnp.int32, (C, 1), 0)
    row = lax.broadcasted_iota(jnp.int32, (C, C), 0)
    col = lax.broadcasted_iota(jnp.int32, (C, C), 1)
    rel = (row - col) * (1 - 2 * d)
    strict = (rel > 0)[None]
    incl = (rel >= 0)[None]
    eye = (row == col).astype(F32)[None]
    tri = jnp.where(rel >= 0, 1.0, 0.0).astype(BF16)
    mm2 = (((1,), (0,)), ((), ()))

    for bi in range(nseq):
        z = jnp.where(is_ctx, zc_ref[bi], zl_ref[bi])
        prev_row = jnp.where(is_ctx, zcp_ref[bi, SUBLANE - 1:SUBLANE, :], zlp_ref[bi, SUBLANE - 1:SUBLANE, :])
        next_row = jnp.where(is_ctx, zcn_ref[bi, 0:1, :], zln_ref[bi, 0:1, :])
        prev_row = jnp.where(first, 0.0, prev_row)
        next_row = jnp.where(last, 0.0, next_row)
        zp = jnp.where(trow == 0, prev_row, pltpu.roll(z, 1, axis=0))
        zn = jnp.where(trow == C - 1, next_row, pltpu.roll(z, C - 1, axis=0))
        u_ref[bi] = z + mu_ref[0:1, :] * (zp - z) + mu_ref[1:2, :] * (zn - z)
        gs_ref[0, bi] = u_ref[bi, :, g_off:g_off + 2 * LANE]

        lora_w = u_ref[bi, :, 3 * rw:3 * rw + 2 * R_LORA_W]
        lora_a = u_ref[bi, :, 3 * rw + 2 * R_LORA_W:3 * rw + 2 * R_LORA_W + 2 * R_LORA_A]
        w_log = w0_ref[0] + _dot3(jnp.tanh(lora_w), wup_ref[0], mm2)
        nx = -w_log
        softplus = jnp.maximum(nx, 0.0) + jnp.log(1.0 + jnp.exp(-jnp.abs(nx)))
        lw = -jnp.exp(-softplus - 0.5)
        a = jax.nn.sigmoid(a0_ref[0] + _dot3(lora_a, aup_ref[0], mm2))

        l1 = lw.astype(BF16)
        r1 = lw - l1.astype(F32)
        l2 = r1.astype(BF16)
        l3 = (r1 - l2.astype(F32)).astype(BF16)
        cum = _dg(tri, l1, mm2) + (_dg(tri, l2, mm2) + _dg(tri, l3, mm2))
        g_end = jnp.exp(jnp.sum(lw, axis=0, keepdims=True))
        k = u_ref[bi, :, rw:2 * rw]
        gp_s[bi] = jnp.exp(cum - lw)
        gi_s[bi] = jnp.exp(-cum)
        a_s[bi] = a
        kn_s[bi] = k * kk_ref[...]
        kd_s[bi] = k * (1.0 + (a - 1.0) * ka_ref[...])
        rt_s[bi] = u_ref[bi, :, :rw] * jnp.exp(cum)

        for h in range(nh):
            sl = slice(h * N, (h + 1) * N)
            hh = bi * nh + h
            kk = kn_s[bi, :, sl]
            kk = kk * lax.rsqrt(jnp.maximum(jnp.sum(kk * kk, axis=-1, keepdims=True), 1e-24))
            r_h = u_ref[bi, :, h * N:(h + 1) * N]
            v_h = u_ref[bi, :, 2 * rw + h * N:2 * rw + (h + 1) * N]
            kd_h = kd_s[bi, :, sl]
            at3[hh] = -kk * gp_s[bi, :, sl]
            bt3[hh] = kk * a_s[bi, :, sl] * gi_s[bi, :, sl]
            kt3[hh] = kd_h * gi_s[bi, :, sl]
            rt3[hh] = rt_s[bi, :, sl]
            v3[hh] = v_h
            gc3[hh] = g_end[:, sl]
            bon_ref[0, bi, :, sl] = jnp.sum(r_h * kd_h * rk_ref[:, sl], axis=-1, keepdims=True) * v_h

    at = at3[...]
    bt = bt3[...]
    kt = kt3[...]
    rt = rt3[...]
    v = v3[...]
    gc = gc3[...]
    m0 = st_ref[...]
    bmm = (((2,), (1,)), ((0,), (0,)))
    bmt = (((2,), (2,)), ((0,), (0,)))
    btm = (((1,), (1,)), ((0,), (0,)))

    ar = jnp.concatenate([at, rt], axis=1)
    bk = jnp.concatenate([bt, kt], axis=1)
    big = _dot1(ar, bk, bmt)
    a_ab = jnp.where(strict, big[:, :C, :C], 0.0)
    a_ak = jnp.where(strict, big[:, :C, C:], 0.0)
    a_rb = jnp.where(incl, big[:, C:, :C], 0.0)
    a_rk = jnp.where(incl, big[:, C:, C:], 0.0)

    tinv = eye + a_ab
    p = _dot3(a_ab, a_ab, bmm)
    for _ in range(int(math.log2(C)) - 2):
        both = _dot3(jnp.concatenate([tinv, p], axis=1), p, bmm)
        tinv = tinv + both[:, :C]
        p = both[:, C:]
    tinv = tinv + _dot3(tinv, p, bmm)

    av = _dot1(jnp.concatenate([a_ak, a_rk], axis=1), v, bmm)
    akv = av[:, :C]
    w12 = _dot3(tinv, jnp.concatenate([at, akv], axis=2), bmm)
    rb12 = _dot1(a_rb, w12, bmm)
    rq = rt + rb12[:, :, :at.shape[2]]
    yc = rb12[:, :, at.shape[2]:] + av[:, C:]
    w1 = w12[:, :, :at.shape[2]]
    w2 = w12[:, :, at.shape[2]:]

    um = _dot1(jnp.concatenate([w1, rq], axis=1), m0, bmt)
    u = um[:, :C] + w2
    y3 = um[:, C:] + yc
    for bi in range(nseq):
        for h in range(nh):
            y_ref[0, bi, :, h * N:(h + 1) * N] = y3[bi * nh + h]
    upd = _dot1(jnp.concatenate([u, v], axis=1), bk, btm)
    st_ref[...] = (m0 + upd) * gc


def rwkv_scan(zc, zl, cols, mu, w0, w_up, a0, a_up, k_k, k_a, r_k):
    B, Lc, _ = zc.shape
    Ll = zl.shape[1]
    T = Lc + Ll
    rw = w0.shape[-1]
    C = RWKV_CHUNK
    N = R_HEAD
    H = rw // N
    nc = T // C
    nctx = Lc // C
    sub = C // SUBLANE
    g_off = 3 * rw + 2 * R_LORA_W + 2 * R_LORA_A

    def chunk(d, c):
        return _rwkv_chunk_index(d, c, nctx, nc)

    def seg(d, c, ctx):
        ci = chunk(d, c)
        return jnp.clip(ci, 0, nctx - 1) if ctx else jnp.clip(ci - nctx, 0, nc - nctx - 1)

    nseq = _pick(B, (RWKV_SEQS, 1))

    def main(ctx):
        return pl.BlockSpec((nseq, C, cols), lambda d, b, c: (b, seg(d, c, ctx), 0))

    def halo(ctx, after):
        nrow = (Lc if ctx else Ll) // SUBLANE

        def index(d, b, c):
            first = seg(d, c, ctx) * sub
            return (b, jnp.clip(first + sub if after else first - 1, 0, nrow - 1), 0)
        return pl.BlockSpec((nseq, SUBLANE, cols), index)

    def widen(w):
        z = jnp.zeros_like(w[0])
        return jnp.stack([jnp.concatenate([w[0], z], axis=0), jnp.concatenate([z, w[1]], axis=0)])

    dir_vec = pl.BlockSpec((1, 1, rw), lambda d, b, c: (d, 0, 0))
    dir_mat = pl.BlockSpec((1, 2 * R_LORA_W, rw), lambda d, b, c: (d, 0, 0))
    vec = pl.BlockSpec((1, rw), lambda d, b, c: (0, 0))
    out_spec = pl.BlockSpec((1, nseq, C, rw), lambda d, b, c: (d, b, chunk(d, c), 0))
    gs_spec = pl.BlockSpec((1, nseq, C, 2 * LANE), lambda d, b, c: (d, b, chunk(d, c), 0))
    full = pltpu.VMEM((nseq, C, rw), F32)
    per_head = pltpu.VMEM((nseq * H, C, N), F32)
    mu_pad = jnp.pad(mu, ((0, 0), (0, cols - mu.shape[1])))
    return pl.pallas_call(
        functools.partial(_rwkv_body, rw=rw, nctx=nctx, g_off=g_off),
        grid=(2, B // nseq, nc),
        in_specs=[main(True), halo(True, False), halo(True, True),
                  main(False), halo(False, False), halo(False, True),
                  pl.BlockSpec((2, cols), lambda d, b, c: (0, 0)),
                  dir_vec, dir_mat, dir_vec, dir_mat, vec, vec, vec],
        out_specs=[out_spec, out_spec, gs_spec],
        out_shape=[jax.ShapeDtypeStruct((2, B, T, rw), F32), jax.ShapeDtypeStruct((2, B, T, rw), F32),
                   jax.ShapeDtypeStruct((2, B, T, 2 * LANE), F32)],
        scratch_shapes=[pltpu.VMEM((nseq * H, N, N), F32), full, full, full, full, full, full,
                        per_head, per_head, per_head, per_head, per_head, pltpu.VMEM((nseq * H, 1, N), F32),
                        pltpu.VMEM((nseq, C, cols), F32)],
        compiler_params=pltpu.CompilerParams(
            dimension_semantics=("parallel", "parallel", "arbitrary"), vmem_limit_bytes=VMEM_LIMIT),
    )(zc, zc, zc, zl, zl, zl, mu_pad, w0[:, None, :], widen(w_up), a0[:, None, :], widen(a_up),
      k_k.reshape(1, rw), k_a.reshape(1, rw), r_k.reshape(1, rw))


def _rwkv_readout_body(y_ref, bon_ref, u_ref, gup_ref, lw_ref, lb_ref, o_ref, *, rw):
    N = R_HEAD
    ys = y_ref[0, 0] + y_ref[1, 0]
    gate = _dot3(jax.nn.sigmoid(u_ref[0, 0]), gup_ref[...], (((1,), (0,)), ((), ())))
    extra = bon_ref[0, 0] + bon_ref[1, 0] + lb_ref[...]
    for h in range(rw // N):
        sl = slice(h * N, (h + 1) * N)
        t = ys[:, sl]
        mean = jnp.mean(t, axis=-1, keepdims=True)
        var = jnp.mean(jnp.square(t - mean), axis=-1, keepdims=True)
        yn = (t - mean) * lax.rsqrt(var + R_GN_EPS) * lw_ref[:, sl]
        o_ref[0, :, sl] = (yn + extra[:, sl]) * gate[:, sl]


def rwkv_readout(y, bonus, gs, g_up, lnx_w, lnx_b):
    _, B, T, rw = y.shape
    tm = _pick(T, (256, 128, 64))
    gw = gs.shape[-1]
    g_pad = jnp.pad(g_up, ((0, gw - g_up.shape[0]), (0, 0)))
    pair = pl.BlockSpec((2, 1, tm, rw), lambda b, i: (0, b, i, 0))
    vec = pl.BlockSpec((1, rw), lambda b, i: (0, 0))
    return pl.pallas_call(
        functools.partial(_rwkv_readout_body, rw=rw),
        grid=(B, T // tm),
        in_specs=[pair, pair, pl.BlockSpec((1, 1, tm, gw), lambda b, i: (0, b, i, 0)),
                  pl.BlockSpec((gw, rw), lambda b, i: (0, 0)), vec, vec],
        out_specs=pl.BlockSpec((1, tm, rw), lambda b, i: (b, i, 0)),
        out_shape=jax.ShapeDtypeStruct((B, T, rw), F32),
        compiler_params=pltpu.CompilerParams(
            dimension_semantics=("parallel", "parallel"), vmem_limit_bytes=VMEM_LIMIT),
    )(y, bonus, gs, g_pad, lnx_w.reshape(1, rw), lnx_b.reshape(1, rw))


def _moe_up_body(be_ref, nu_ref, src_ref, nxt_ref, h_hbm, wg_ref, wu_ref, o_ref,
                 wgs_ref, wus_ref, rows_ref, xs_ref, sem):
    i = pl.program_id(0)
    nblk = pl.num_programs(0)
    slot = lax.rem(i, 2)
    prev = be_ref[jnp.maximum(i - 1, 0)]
    fresh = jnp.logical_or(i == 0, be_ref[i] != prev)

    def row_copy(idx_ref, r, dst_slot):
        return pltpu.make_async_copy(h_hbm.at[pl.ds(idx_ref[0, 0, r], 1), :],
                                     rows_ref.at[dst_slot, pl.ds(r, 1), :], sem.at[dst_slot])

    def request(idx_ref, dst_slot):
        for r in range(MOE_BLOCK):
            row_copy(idx_ref, r, dst_slot).start()

    def drain(dst_slot):
        for r in range(MOE_BLOCK):
            pltpu.make_async_copy(h_hbm.at[pl.ds(0, 1), :], rows_ref.at[dst_slot, pl.ds(r, 1), :],
                                  sem.at[dst_slot]).wait()

    @pl.when(i == 0)
    def _():
        request(src_ref, 0)

    @pl.when(fresh)
    def _():
        wgs_ref[...] = wg_ref[0, 0].astype(BF16)
        wus_ref[...] = wu_ref[0, 0].astype(BF16)

    drain(slot)
    xs_ref[...] = rows_ref[slot].astype(BF16)

    @pl.when(i < nu_ref[0])
    def _():
        request(nxt_ref, 1 - slot)
        x = xs_ref[...]
        g = jnp.dot(x, wgs_ref[...], preferred_element_type=F32)
        u = jnp.dot(x, wus_ref[...], preferred_element_type=F32)
        o_ref[...] = (g * jax.nn.sigmoid(g) * u).astype(o_ref.dtype)

    @pl.when(i >= nu_ref[0])
    def _():
        request(nxt_ref, 1 - slot)
        o_ref[...] = jnp.zeros_like(o_ref)

    @pl.when(i == nblk - 1)
    def _():
        drain(1 - slot)


def _moe_down_body(be_ref, nu_ref, h_ref, wd_ref, o_ref, wds_ref):
    i = pl.program_id(0)
    prev = be_ref[jnp.maximum(i - 1, 0)]
    fresh = jnp.logical_or(i == 0, be_ref[i] != prev)

    @pl.when(fresh)
    def _():
        wds_ref[...] = wd_ref[0, 0].astype(BF16)

    @pl.when(i < nu_ref[0])
    def _():
        o_ref[...] = jnp.dot(h_ref[...], wds_ref[...], preferred_element_type=F32).astype(o_ref.dtype)

    @pl.when(i >= nu_ref[0])
    def _():
        o_ref[...] = jnp.zeros_like(o_ref)


def moe_experts(h, src, block_expert, n_used, w_gate, w_up, w_down, layer):
    D = h.shape[1]
    R = src.shape[0]
    Hd = w_gate.shape[-1]
    nblk = R // MOE_BLOCK
    idx = src.reshape(nblk, 1, MOE_BLOCK)
    hid = pl.pallas_call(
        _moe_up_body,
        grid_spec=pltpu.PrefetchScalarGridSpec(
            num_scalar_prefetch=2,
            grid=(nblk,),
            in_specs=[pl.BlockSpec((1, 1, MOE_BLOCK), lambda i, be, nu: (i, 0, 0), memory_space=pltpu.SMEM),
                      pl.BlockSpec((1, 1, MOE_BLOCK), lambda i, be, nu: (jnp.minimum(i + 1, nblk - 1), 0, 0),
                                   memory_space=pltpu.SMEM),
                      pl.BlockSpec(memory_space=pl.ANY),
                      pl.BlockSpec((1, 1, D, Hd), lambda i, be, nu: (layer, be[i], 0, 0)),
                      pl.BlockSpec((1, 1, D, Hd), lambda i, be, nu: (layer, be[i], 0, 0))],
            out_specs=pl.BlockSpec((MOE_BLOCK, Hd), lambda i, be, nu: (i, 0)),
            scratch_shapes=[pltpu.VMEM((D, Hd), BF16), pltpu.VMEM((D, Hd), BF16),
                            pltpu.VMEM((2, MOE_BLOCK, D), F32), pltpu.VMEM((MOE_BLOCK, D), BF16),
                            pltpu.SemaphoreType.DMA((2,))]),
        out_shape=jax.ShapeDtypeStruct((R, Hd), BF16),
        compiler_params=pltpu.CompilerParams(
            dimension_semantics=("arbitrary",), vmem_limit_bytes=VMEM_LIMIT),
    )(block_expert, n_used, idx, idx, h, w_gate, w_up)
    return pl.pallas_call(
        _moe_down_body,
        grid_spec=pltpu.PrefetchScalarGridSpec(
            num_scalar_prefetch=2,
            grid=(nblk,),
            in_specs=[pl.BlockSpec((MOE_BLOCK, Hd), lambda i, be, nu: (i, 0)),
                      pl.BlockSpec((1, 1, Hd, D), lambda i, be, nu: (layer, be[i], 0, 0))],
            out_specs=pl.BlockSpec((MOE_BLOCK, D), lambda i, be, nu: (i, 0)),
            scratch_shapes=[pltpu.VMEM((Hd, D), BF16)]),
        out_shape=jax.ShapeDtypeStruct((R, D), F32),
        compiler_params=pltpu.CompilerParams(
            dimension_semantics=("arbitrary",), vmem_limit_bytes=VMEM_LIMIT),
    )(block_expert, n_used, hid, w_down)


def moe_layer(h, logits, w_gate, w_up, w_down, layer):
    T, D = h.shape
    lg = logits[:, :N_GROUPS]
    g_idx = jnp.argmax(lg, axis=-1)
    p_group = jnp.take_along_axis(jax.nn.softmax(lg, axis=-1), g_idx[:, None], axis=-1)
    le = logits[:, N_GROUPS:N_GROUPS + N_EXPERTS].reshape(T, N_GROUPS, EXPERTS_PER_GROUP)
    le_sel = jnp.take_along_axis(le, g_idx[:, None, None], axis=1)[:, 0]
    top_v, top_i = lax.top_k(le_sel, TOP_K)
    gate = p_group * jax.nn.softmax(top_v, axis=-1)
    expert = (g_idx[:, None] * EXPERTS_PER_GROUP + top_i).astype(jnp.int32)

    A = T * TOP_K
    e_flat = expert.reshape(A)
    onehot = (e_flat[:, None] == jnp.arange(N_EXPERTS, dtype=jnp.int32)[None, :]).astype(jnp.int32)
    rank = jnp.take_along_axis(jnp.cumsum(onehot, axis=0), e_flat[:, None], axis=1)[:, 0] - 1
    counts = jnp.sum(onehot, axis=0)
    padded = (counts + MOE_BLOCK - 1) // MOE_BLOCK * MOE_BLOCK
    pad_end = jnp.cumsum(padded)
    pad_start = pad_end - padded
    dest = pad_start[e_flat] + rank
    nblk = -(-A // MOE_BLOCK) + N_EXPERTS
    R = nblk * MOE_BLOCK
    n_used = (pad_end[-1] // MOE_BLOCK).astype(jnp.int32)
    blk = jnp.arange(nblk, dtype=jnp.int32)
    be = jnp.sum((pad_end[None, :] <= (blk * MOE_BLOCK)[:, None]).astype(jnp.int32), axis=1)
    be = jnp.minimum(be, N_EXPERTS - 1)
    be = jnp.where(blk < n_used, be, be[jnp.maximum(n_used - 1, 0)])
    src = jnp.zeros((R,), jnp.int32).at[dest].set(jnp.arange(A, dtype=jnp.int32) // TOP_K)
    ys = moe_experts(h, src, be, n_used.reshape(1), w_gate, w_up, w_down, layer)
    d2 = dest.reshape(T, TOP_K)
    return (gate[:, 0:1] * jnp.take(ys, d2[:, 0], axis=0, mode='clip')
            + gate[:, 1:2] * jnp.take(ys, d2[:, 1], axis=0, mode='clip'))


def _rope_tables(L):
    half = A_HEAD_DIM // 2
    inv_freq = ROPE_BASE ** (-jnp.arange(0, half, 2, dtype=F32) / half)
    t = jnp.arange(L)
    row = (t // GRID_W).astype(F32)
    col = (t % GRID_W).astype(F32)
    ar = row[:, None] * inv_freq[None, :]
    ac = col[:, None] * inv_freq[None, :]
    cos = jnp.concatenate([jnp.cos(ar), jnp.cos(ar), jnp.cos(ac), jnp.cos(ac)], axis=-1)
    sin = jnp.concatenate([-jnp.sin(ar), jnp.sin(ar), -jnp.sin(ac), jnp.sin(ac)], axis=-1)
    reps = LANE // A_HEAD_DIM
    return jnp.tile(cos, (1, reps)), jnp.tile(sin, (1, reps))


def _rwkv_mixer(zc, zl, cols, mu, w0, w_up, a0, a_up, g_up, k_k, k_a, r_k, lnx_w, lnx_b):
    y, bonus, gs = rwkv_scan(zc, zl, cols, mu, w0, w_up, a0, a_up, k_k, k_a, r_k)
    return rwkv_readout(y, bonus, gs, g_up, lnx_w, lnx_b)


def _dft_table(L):
    N = 2 * L
    half = DFT_TILE // 2
    k = jnp.arange(L, dtype=jnp.int32)
    n = jnp.arange(L, dtype=jnp.int32)
    m = ((2 * k + 1)[:, None] * n[None, :]) % (2 * N)
    ang = m.astype(F32) * (math.pi / N)
    tab = jnp.stack([jnp.cos(ang).reshape(L // half, half, L), (-jnp.sin(ang)).reshape(L // half, half, L)], axis=1)
    return tab.reshape(2 * L, L)


def _hyena_filters(L, w1, b1, f1, w2, b2, f2, w3, width):
    hp = lax.Precision.HIGHEST
    t = jnp.linspace(0.0, 1.0, L, dtype=F32)[:, None]
    bands = jnp.linspace(1e-4, HY_BANDS - 1, HY_BANDS, dtype=F32)[None, :]
    ang = (2 * math.pi) * jnp.arange(L, dtype=F32)[:, None] / L * bands
    z = jnp.concatenate([t, jnp.cos(ang), -jnp.sin(ang)], axis=-1)
    h = jnp.sin(f1 * (jnp.dot(z, w1, precision=hp) + b1))
    h = jnp.sin(f2 * (jnp.dot(h, w2, precision=hp) + b2))
    h = jnp.dot(h, w3, precision=hp)
    deltas = jnp.abs(jnp.linspace(HY_MIN_DECAY, HY_MAX_DECAY, width, dtype=F32))
    h = h * jnp.exp(-t * jnp.tile(deltas, 2)[None, :])
    row0 = (jnp.arange(L) > 0).astype(F32)[:, None]
    h_bwd = h[:, width:] * row0
    h = jnp.concatenate([h[:, :width], h_bwd], axis=1)
    norm = jnp.sum(jnp.abs(h[:, :width]), axis=0, keepdims=True) + jnp.sum(jnp.abs(h_bwd), axis=0, keepdims=True)
    return h, norm


def _hy_pre_body(z_ref, zp_ref, zn_ref, w_ref, b_ref, hb_ref, u_ref, t_ref, x0_ref, *, width):
    i = pl.program_id(1)
    last = pl.num_programs(1) - 1
    z = z_ref[0]
    tl = z.shape[0]
    prev_row = jnp.where(i > 0, zp_ref[0, SUBLANE - 1:SUBLANE, :], 0.0)
    next_row = jnp.where(i < last, zn_ref[0, 0:1, :], 0.0)
    row = lax.broadcasted_iota(jnp.int32, (tl, 1), 0)
    up = jnp.where(row == 0, prev_row, pltpu.roll(z, 1, axis=0))
    dn = jnp.where(row == tl - 1, next_row, pltpu.roll(z, tl - 1, axis=0))
    c = up * w_ref[0:1, :] + z * w_ref[1:2, :] + dn * w_ref[2:3, :] + b_ref[...]
    u = c[:, 2 * width:] * c[:, width:2 * width]
    u_ref[0] = u.astype(u_ref.dtype)
    t_ref[0] = u * hb_ref[...]
    x0_ref[0] = c[:, :width]


def hyena_pre(z, conv_w, conv_b, bias):
    B, L, W3 = z.shape
    W = W3 // 3
    tl = _pick(L, (256, 128, 64, 32, 16, 8))
    nsub = tl // SUBLANE
    out = pl.BlockSpec((1, tl, W), lambda b, i: (b, i, 0))
    return pl.pallas_call(
        functools.partial(_hy_pre_body, width=W),
        grid=(B, L // tl),
        in_specs=[pl.BlockSpec((1, tl, W3), lambda b, i: (b, i, 0)),
                  pl.BlockSpec((1, SUBLANE, W3), lambda b, i: (b, jnp.maximum(i * nsub - 1, 0), 0)),
                  pl.BlockSpec((1, SUBLANE, W3), lambda b, i: (b, jnp.minimum((i + 1) * nsub, L // SUBLANE - 1), 0)),
                  pl.BlockSpec((3, W3), lambda b, i: (0, 0)),
                  pl.BlockSpec((1, W3), lambda b, i: (0, 0)),
                  pl.BlockSpec((1, W), lambda b, i: (0, 0))],
        out_specs=[out, out, out],
        out_shape=[jax.ShapeDtypeStruct((B, L, W), BF16), jax.ShapeDtypeStruct((B, L, W), F32),
                   jax.ShapeDtypeStruct((B, L, W), F32)],
        compiler_params=pltpu.CompilerParams(
            dimension_semantics=("parallel", "parallel"), vmem_limit_bytes=VMEM_LIMIT),
    )(z, z, z, conv_w, conv_b.reshape(1, W3), bias.reshape(1, W))


def _hyena_core(z, conv_w, conv_b, filt, bias):
    B, L, W3 = z.shape
    W = W3 // 3
    h, norm = filt
    u, t, x0 = hyena_pre(z, conv_w, conv_b, bias)
    tab = _dft_table(L)
    fwd = tab.astype(BF16)
    inv = tab.T.astype(BF16)
    hf = mm(fwd, h)
    half = DFT_TILE // 2
    sign = jnp.tile(jnp.concatenate([jnp.ones((half, 1), F32), -jnp.ones((half, 1), F32)]), (2 * L // DFT_TILE, 1))
    kf = (hf[:, :W] + sign * hf[:, W:]) / norm
    yf = mm(fwd, u, cmul=kf, out_dtype=BF16, nb=B)
    return mm(inv, yf, post=(t, x0, 1.0 / L), nb=B)


def kernel(x, c, ctx, c_ctx, mod_w, mod_b, norm1_w, norm2_w, router_group_w, router_expert_w, expert_w_gate,
           expert_w_up, expert_w_down, mix_w_in, mix_w_out, attn_sink, rwkv_shift_mu, rwkv_w0, rwkv_w_up,
           rwkv_a0, rwkv_a_up, rwkv_g_up, rwkv_k_k, rwkv_k_a, rwkv_r_k, rwkv_lnx_w, rwkv_lnx_b, hy_w_in,
           hy_b_in, hy_conv_w, hy_conv_b, hy_ffn_w1, hy_ffn_b1, hy_sin_f1, hy_ffn_w2, hy_ffn_b2, hy_sin_f2,
           hy_ffn_w3, hy_bias, hy_w_out, final_norm_w):
    B, L, D = x.shape
    Cn = ctx.shape[1]
    depth = mod_w.shape[0]
    a_q = D // 2
    a_kv = a_q // A_HEAD_DIM // A_GROUP * A_HEAD_DIM
    a_cols = a_q + 2 * a_kv
    rw = D // 2
    in_cols = mix_w_in.shape[-1]
    r_cols = -(-(in_cols - a_cols) // (2 * LANE)) * (2 * LANE)
    q_off = -(-r_cols // a_q) * a_q
    in_pad = q_off + a_cols
    assert in_pad % (2 * LANE) == 0

    cc = jnp.concatenate([c, c_ctx[None], jnp.zeros((SUBLANE - B - 1, D), F32)], axis=0)
    sc = jax.nn.silu(cc)
    mods = [mm(sc, mod_w, w_idx=i, bias=mod_b[i]).reshape(SUBLANE, 6, D) for i in range(depth)]
    cos, sin = _rope_tables(L)
    rope = (cos, sin, cos, sin)

    def gated_add(r, gate, f):
        n = r.shape[0] // B
        return (r.reshape(B, n, D) + gate[:, None, :] * f.reshape(B, n, D)).reshape(B * n, D)

    xl = x.reshape(B * L, D)
    xc = ctx.reshape(B * Cn, D)
    for i in range(depth):
        ctx_later = any(j % 2 == 0 for j in range(i + 1, depth))
        ctx_here = ctx_later or (i % 2 == 0)
        ml_ = mods[i][:B]
        mc_ = jnp.broadcast_to(mods[i][B:B + 1], (B, 6, D))

        def nrm(m, gain, a, b):
            return ((gain * (1 + m[:, b]))[:, None, :], m[:, a][:, None, :])

        n1l = nrm(ml_, norm1_w[i], 0, 1)
        n1c = nrm(mc_, norm1_w[i], 0, 1)
        gate_l = ml_[:, 2][:, None, :]
        gate_c = mc_[:, 2][:, None, :]
        if i % 2 == 0:
            e = i // 2
            wm = mix_w_in[e]
            w_in = jnp.concatenate([wm[:, a_cols:], jnp.zeros((D, q_off - (in_cols - a_cols)), F32),
                                    wm[:, :a_cols]], axis=1).astype(BF16)
            zl = mm(xl, w_in, norm=n1l, rows_per_group=L, tn=in_pad // 2).reshape(B, L, in_pad)
            zc = mm(xc, w_in, norm=n1c, rows_per_group=Cn, tn=in_pad // 2).reshape(B, Cn, in_pad)
            att_l = attention(zl, zc, attn_sink[e], rope, a_q=a_q, a_kv=a_kv, q_off=q_off, band=True)
            rw_all = _rwkv_mixer(zc, zl, r_cols, rwkv_shift_mu[e], rwkv_w0[e], rwkv_w_up[e], rwkv_a0[e],
                                 rwkv_a_up[e], rwkv_g_up[e], rwkv_k_k[e], rwkv_k_a[e], rwkv_r_k[e],
                                 rwkv_lnx_w[e], rwkv_lnx_b[e])
            w_out = mix_w_out[e].astype(BF16)
            xl = mm(att_l.reshape(B * L, a_q), w_out, x2=rw_all[:, Cn:].reshape(B * L, rw),
                    res=(xl, gate_l), rows_per_group=L)
            if ctx_later:
                att_c = attention(zl, zc, attn_sink[e], None, a_q=a_q, a_kv=a_kv, q_off=q_off, band=False)
                xc = mm(att_c.reshape(B * Cn, a_q), w_out, x2=rw_all[:, :Cn].reshape(B * Cn, rw),
                        res=(xc, gate_c), rows_per_group=Cn)
        else:
            o = i // 2
            w_in = hy_w_in[o].astype(BF16)
            w_out = hy_w_out[o].astype(BF16)
            fl = (hy_ffn_w1[o], hy_ffn_b1[o], hy_sin_f1[o], hy_ffn_w2[o], hy_ffn_b2[o], hy_sin_f2[o], hy_ffn_w3[o])
            zl = mm(xl, w_in, norm=n1l, bias=hy_b_in[o], rows_per_group=L).reshape(B, L, 3 * D)
            gl = _hyena_core(zl, hy_conv_w[o], hy_conv_b[o], _hyena_filters(L, *fl, D), hy_bias[o])
            xl = mm(gl.reshape(B * L, D), w_out, res=(xl, gate_l), rows_per_group=L)
            if ctx_later:
                zc = mm(xc, w_in, norm=n1c, bias=hy_b_in[o], rows_per_group=Cn).reshape(B, Cn, 3 * D)
                gc = _hyena_core(zc, hy_conv_w[o], hy_conv_b[o], _hyena_filters(Cn, *fl, D), hy_bias[o])
                xc = mm(gc.reshape(B * Cn, D), w_out, res=(xc, gate_c), rows_per_group=Cn)

        w_route = jnp.concatenate([router_group_w[i], router_expert_w[i]], axis=1)
        w_route = jnp.pad(w_route, ((0, 0), (0, LANE - w_route.shape[1])))
        n2l = nrm(ml_, norm2_w[i], 3, 4)
        lg_l, h_l = mm(xl, w_route, norm=n2l, rows_per_group=L, x3=True, emit_h=F32)
        if ctx_later:
            n2c = nrm(mc_, norm2_w[i], 3, 4)
            lg_c, h_c = mm(xc, w_route, norm=n2c, rows_per_group=Cn, x3=True, emit_h=F32)
            f = moe_layer(jnp.concatenate([h_l, h_c], axis=0), jnp.concatenate([lg_l, lg_c], axis=0),
                          expert_w_gate, expert_w_up, expert_w_down, i)
            xl = gated_add(xl, ml_[:, 5], f[:B * L])
            xc = gated_add(xc, mc_[:, 5], f[B * L:])
        else:
            f = moe_layer(h_l, lg_l, expert_w_gate, expert_w_up, expert_w_down, i)
            xl = gated_add(xl, ml_[:, 5], f)

    xf = xl.reshape(B, L, D)
    y = xf * lax.rsqrt(jnp.mean(xf * xf, axis=-1, keepdims=True) + NORM_EPS)
    return y * final_norm_w
```

```python
import functools
import math

import jax
import jax.numpy as jnp
from jax import lax
from jax.experimental import pallas as pl
from jax.experimental.pallas import tpu as pltpu

F32 = jnp.float32
BF16 = jnp.bfloat16

GRID_W = 64
NORM_EPS = 1e-6
A_HEAD_DIM = 64
A_GROUP = 8
A_WINDOW = 128
A_BLOCK = 128
ROPE_BASE = 10000.0
NEG_INF = -1e30
R_HEAD = 64
R_LORA_W = 64
R_LORA_A = 64
R_GN_EPS = 64e-5
HY_BANDS = 16
HY_MIN_DECAY = math.log(1e-2) / 1.5
HY_MAX_DECAY = math.log(1e-2) / 0.3
N_GROUPS = 4
EXPERTS_PER_GROUP = 8
N_EXPERTS = N_GROUPS * EXPERTS_PER_GROUP
TOP_K = 2

LANE = 128
SUBLANE = 8
VMEM_LIMIT = 56 * 1024 * 1024
MM_TILE_BYTES = 8 * 1024 * 1024
DFT_TILE = 512
RWKV_CHUNK = 64
RWKV_SEQS = 2
MOE_BLOCK = 256


def _pick(n, cands):
    for c in cands:
        if n % c == 0:
            return c
    return n


def _split_bf16(x):
    hi = x.astype(BF16)
    lo = (x - hi.astype(F32)).astype(BF16)
    return hi, lo


def _dg(a, b, dims):
    return lax.dot_general(a, b, dims, preferred_element_type=F32)


def _dot3(a, b, dims):
    ah, al = _split_bf16(a)
    bh, bl = _split_bf16(b)
    ax = a.ndim - 2
    m = a.shape[ax]
    if dims[0][0][0] == ax:
        return _dg(ah, bh, dims) + (_dg(ah, bl, dims) + _dg(al, bh, dims))
    both = _dg(jnp.concatenate([ah, al], axis=ax), bh, dims)
    return (lax.slice_in_dim(both, 0, m, axis=ax) + lax.slice_in_dim(both, m, 2 * m, axis=ax)) + _dg(ah, bl, dims)


def _dot1(a, b, dims):
    return _dg(a.astype(BF16), b.astype(BF16), dims)


def _mm_body(*refs, norm, bias, res, x3, emit_h, w3d, two_x, cmul, post, batched):
    it = iter(refs)
    x_ref = next(it)
    x2_ref = next(it) if two_x else None
    w_ref = next(it)
    g_ref = next(it) if norm else None
    s_ref = next(it) if norm else None
    b_ref = next(it) if bias else None
    r_ref = next(it) if res else None
    gt_ref = next(it) if res else None
    k_ref = next(it) if cmul else None
    pt_ref = next(it) if post else None
    px_ref = next(it) if post else None
    o_ref = next(it)
    h_ref = next(it) if emit_h else None
    xh_ref = next(it)
    xl_ref = next(it) if x3 else None

    @pl.when(pl.program_id(2 if batched else 1) == 0)
    def _():
        x = x_ref[...].astype(F32)
        if norm:
            ms = jnp.mean(x * x, axis=-1, keepdims=True)
            x = x * lax.rsqrt(ms + NORM_EPS) * g_ref[0] + s_ref[0]
        xh = x.astype(BF16)
        if two_x:
            xh_ref[:, :x.shape[1]] = xh
            xh_ref[:, x.shape[1]:] = x2_ref[...].astype(BF16)
        else:
            xh_ref[...] = xh
        if x3:
            xl_ref[...] = (x - xh.astype(F32)).astype(BF16)
        if emit_h:
            h_ref[...] = x.astype(h_ref.dtype)

    w = w_ref[0] if (w3d or batched) else w_ref[...]
    wh = w.astype(BF16)
    acc = jnp.dot(xh_ref[...], wh, preferred_element_type=F32)
    if x3:
        wl = (w.astype(F32) - wh.astype(F32)).astype(BF16)
        acc = acc + (jnp.dot(xh_ref[...], wl, preferred_element_type=F32)
                     + jnp.dot(xl_ref[...], wh, preferred_element_type=F32))
    if bias:
        acc = acc + b_ref[...]
    if res:
        acc = r_ref[...] + gt_ref[0] * acc
    if cmul:
        h = acc.shape[0] // 2
        re, im = acc[:h], acc[h:]
        kr, ki = k_ref[:h, :], k_ref[h:, :]
        acc = jnp.concatenate([re * kr - im * ki, re * ki + im * kr], axis=0)
    if post:
        acc = (acc * post + (pt_ref[0] if batched else pt_ref[...])) * (px_ref[0] if batched else px_ref[...])
    if batched:
        o_ref[0] = acc.astype(o_ref.dtype)
    else:
        o_ref[...] = acc.astype(o_ref.dtype)


def _mm_tiles(M, K, N, rows_per_group, w_itemsize):
    tm = _pick(math.gcd(M, rows_per_group), (512, 256, 128, 64, 32, 16, 8))
    cands = [t for t in (2048, 1536, 1024, 768, 640, 512, 384, 256, 128)
             if K * t * w_itemsize <= MM_TILE_BYTES and tm * t * 4 <= MM_TILE_BYTES]
    tn = N if (N <= cands[0] and N % LANE == 0) else _pick(N, cands)
    return tm, tn


def mm(x, w, *, x2=None, w_idx=None, norm=None, bias=None, res=None, rows_per_group=None, x3=False,
       emit_h=None, out_dtype=F32, tn=None, cmul=None, post=None, nb=None):
    M, K1 = x.shape
    K = K1 if x2 is None else K1 + x2.shape[1]
    assert x2 is None or (norm is None and not x3 and emit_h is None)
    N = w.shape[-1]
    if rows_per_group is None:
        rows_per_group = M
    tm, tn_auto = _mm_tiles(M, K, N, rows_per_group, w.dtype.itemsize)
    tn = tn_auto if tn is None else tn
    assert M % tm == 0 and N % tn == 0 and rows_per_group % tm == 0
    bpg = rows_per_group // tm
    assert nb is None or (w_idx is None and x2 is None and norm is None and bias is None and res is None
                          and emit_h is None)

    def shared(shape, f):
        return pl.BlockSpec(shape, f if nb is None else (lambda b, i, j: f(i, j)))

    def per_problem(shape, f):
        if nb is None:
            return pl.BlockSpec(shape, f)
        return pl.BlockSpec((1,) + shape, lambda b, i, j: (b,) + f(i, j))

    if w_idx is None:
        w_spec = per_problem((K, tn), lambda i, j: (0, j))
    else:
        w_spec = pl.BlockSpec((1, K, tn), lambda i, j: (w_idx, 0, j))
    in_specs = [shared((tm, K1), lambda i, j: (i, 0))]
    args = [x]
    if x2 is not None:
        in_specs.append(pl.BlockSpec((tm, K - K1), lambda i, j: (i, 0)))
        args.append(x2)
    in_specs.append(w_spec)
    args.append(w)
    if norm is not None:
        in_specs += [pl.BlockSpec((1, 1, K), lambda i, j: (i // bpg, 0, 0))] * 2
        args += [norm[0], norm[1]]
    if bias is not None:
        in_specs.append(pl.BlockSpec((1, tn), lambda i, j: (0, j)))
        args.append(bias.reshape(1, N).astype(F32))
    if res is not None:
        in_specs.append(pl.BlockSpec((tm, tn), lambda i, j: (i, j)))
        in_specs.append(pl.BlockSpec((1, 1, tn), lambda i, j: (i // bpg, 0, j)))
        args += [res[0], res[1]]
    if cmul is not None:
        in_specs.append(shared((tm, tn), lambda i, j: (i, j)))
        args.append(cmul)
    if post is not None:
        in_specs += [per_problem((tm, tn), lambda i, j: (i, j))] * 2
        args += [post[0], post[1]]
    out_shape = [jax.ShapeDtypeStruct((M, N) if nb is None else (nb, M, N), out_dtype)]
    out_specs = [per_problem((tm, tn), lambda i, j: (i, j))]
    if emit_h is not None:
        out_shape.append(jax.ShapeDtypeStruct((M, K), emit_h))
        out_specs.append(pl.BlockSpec((tm, K), lambda i, j: (i, 0)))
    scratch = [pltpu.VMEM((tm, K), BF16)]
    if x3:
        scratch.append(pltpu.VMEM((tm, K), BF16))
    body = functools.partial(_mm_body, norm=norm is not None, bias=bias is not None,
                             res=res is not None, x3=x3, emit_h=emit_h is not None,
                             w3d=w_idx is not None, two_x=x2 is not None, cmul=cmul is not None,
                             post=None if post is None else post[2], batched=nb is not None)
    grid = (M // tm, N // tn)
    sem = ("parallel", "arbitrary")
    outs = pl.pallas_call(
        body,
        grid=grid if nb is None else (nb,) + grid,
        in_specs=in_specs,
        out_specs=out_specs,
        out_shape=out_shape,
        scratch_shapes=scratch,
        compiler_params=pltpu.CompilerParams(
            dimension_semantics=sem if nb is None else ("parallel",) + sem, vmem_limit_bytes=VMEM_LIMIT),
    )(*args)
    return outs if emit_h is not None else outs[0]


def _swap16(x):
    q4 = A_HEAD_DIM // 4
    lane = lax.broadcasted_iota(jnp.int32, x.shape, 1)
    return jnp.where((lane & q4) == 0, pltpu.roll(x, LANE - q4, axis=1), pltpu.roll(x, q4, axis=1))


def _rope_tile(x, cos, sin):
    parts = []
    for j in range(x.shape[1] // LANE):
        t = x[:, j * LANE:(j + 1) * LANE]
        parts.append(t * cos + _swap16(t) * sin)
    return parts[0] if len(parts) == 1 else jnp.concatenate(parts, axis=1)


def _attn_body(*refs, seq, n_kv, band):
    dh = A_HEAD_DIM
    if band:
        (sink_ref, q_ref, k_ref, v_ref, kc_ref, vc_ref, cq_ref, sq_ref, ck_ref, sk_ref,
         o_ref, qs_ref, ks_ref, vs_ref, kcs_ref, vcs_ref) = refs
    else:
        sink_ref, q_ref, kc_ref, vc_ref, o_ref, qs_ref, kcs_ref, vcs_ref = refs
    n = pl.program_id(1)
    scale = dh ** -0.5
    kcs_ref[...] = kc_ref[0].astype(BF16)
    vcs_ref[...] = vc_ref[0].astype(BF16)
    if band:
        win = 3 * A_BLOCK
        start = jnp.clip((n - 1) * A_BLOCK, 0, seq - win)
        start = pl.multiple_of(start, A_BLOCK)
        qpos = n * A_BLOCK + lax.broadcasted_iota(jnp.int32, (A_BLOCK, win), 0)
        kpos = start + lax.broadcasted_iota(jnp.int32, (A_BLOCK, win), 1)
        mask = jnp.abs(kpos - qpos) <= A_WINDOW
        qs_ref[...] = (_rope_tile(q_ref[0], cq_ref[...], sq_ref[...]) * scale).astype(BF16)
        ks_ref[...] = _rope_tile(k_ref[0, pl.ds(start, win), :], ck_ref[pl.ds(start, win), :],
                                 sk_ref[pl.ds(start, win), :]).astype(BF16)
        vs_ref[...] = v_ref[0, pl.ds(start, win), :].astype(BF16)
    else:
        qs_ref[...] = (q_ref[0] * scale).astype(BF16)
    nt = (((1,), (1,)), ((), ()))
    for h in range(n_kv):
        hs = slice(h * dh, (h + 1) * dh)
        kc = kcs_ref[:, hs]
        vc = vcs_ref[:, hs]
        if band:
            kw = ks_ref[:, hs]
            vw = vs_ref[:, hs]
        for g in range(A_GROUP):
            hd = h * A_GROUP + g
            q = qs_ref[:, hd * dh:(hd + 1) * dh]
            sink = sink_ref[hd]
            s_c = _dg(q, kc, nt)
            m = jnp.maximum(jnp.max(s_c, axis=-1, keepdims=True), sink)
            if band:
                s_w = jnp.where(mask, _dg(q, kw, nt), NEG_INF)
                m = jnp.maximum(m, jnp.max(s_w, axis=-1, keepdims=True))
                p_w = jnp.exp(s_w - m)
            p_c = jnp.exp(s_c - m)
            den = jnp.sum(p_c, axis=-1, keepdims=True) + jnp.exp(sink - m)
            o = jnp.dot(p_c.astype(BF16), vc, preferred_element_type=F32)
            if band:
                den = den + jnp.sum(p_w, axis=-1, keepdims=True)
                o = o + jnp.dot(p_w.astype(BF16), vw, preferred_element_type=F32)
            o_ref[0, :, hd * dh:(hd + 1) * dh] = o / den


def attention(z, zc, sink, rope, *, a_q, a_kv, q_off, band):
    src = z if band else zc
    B, L, _ = src.shape
    Cn = zc.shape[1]
    n_kv = a_kv // A_HEAD_DIM
    nb = L // A_BLOCK
    qblk = q_off // a_q
    kblk = (q_off + a_q) // a_kv
    body = functools.partial(_attn_body, seq=L, n_kv=n_kv, band=band)
    q_spec = pl.BlockSpec((1, A_BLOCK, a_q), lambda b, n: (b, n, qblk))
    kc_spec = pl.BlockSpec((1, Cn, a_kv), lambda b, n: (b, 0, kblk))
    vc_spec = pl.BlockSpec((1, Cn, a_kv), lambda b, n: (b, 0, kblk + 1))
    smem = pl.BlockSpec(memory_space=pltpu.SMEM)
    if band:
        cq, sq, ck, sk = rope
        tab_q = pl.BlockSpec((A_BLOCK, LANE), lambda b, n: (n, 0))
        tab_k = pl.BlockSpec((L, LANE), lambda b, n: (0, 0))
        in_specs = [smem, q_spec,
                    pl.BlockSpec((1, L, a_kv), lambda b, n: (b, 0, kblk)),
                    pl.BlockSpec((1, L, a_kv), lambda b, n: (b, 0, kblk + 1)),
                    kc_spec, vc_spec, tab_q, tab_q, tab_k, tab_k]
        args = (sink.astype(F32), z, z, z, zc, zc, cq, sq, ck, sk)
        scratch = [pltpu.VMEM((A_BLOCK, a_q), BF16), pltpu.VMEM((3 * A_BLOCK, a_kv), BF16),
                   pltpu.VMEM((3 * A_BLOCK, a_kv), BF16), pltpu.VMEM((Cn, a_kv), BF16),
                   pltpu.VMEM((Cn, a_kv), BF16)]
    else:
        in_specs = [smem, q_spec, kc_spec, vc_spec]
        args = (sink.astype(F32), zc, zc, zc)
        scratch = [pltpu.VMEM((A_BLOCK, a_q), BF16), pltpu.VMEM((Cn, a_kv), BF16), pltpu.VMEM((Cn, a_kv), BF16)]
    return pl.pallas_call(
        body,
        grid=(B, nb),
        in_specs=in_specs,
        out_specs=pl.BlockSpec((1, A_BLOCK, a_q), lambda b, n: (b, n, 0)),
        out_shape=jax.ShapeDtypeStruct((B, L, a_q), F32),
        scratch_shapes=scratch,
        compiler_params=pltpu.CompilerParams(
            dimension_semantics=("parallel", "arbitrary"), vmem_limit_bytes=VMEM_LIMIT),
    )(*args)


def _rwkv_chunk_index(d, c, nctx, nc):
    back = jnp.where(c < nctx, nctx - 1 - c, nc - 1 - c + nctx)
    return c + d * (back - c)


def _rwkv_body(zc_ref, zcp_ref, zcn_ref, zl_ref, zlp_ref, zln_ref, mu_ref, w0_ref, wup_ref, a0_ref, aup_ref,
               kk_ref, ka_ref, rk_ref, y_ref, bon_ref, gs_ref,
               st_ref, gp_s, gi_s, a_s, kn_s, kd_s, rt_s, at3, bt3, kt3, rt3, v3, gc3, u_ref, *, rw, nctx, g_off):
    C = RWKV_CHUNK
    N = R_HEAD
    nh = rw // N
    nseq = zc_ref.shape[0]
    d = pl.program_id(0)
    nc = pl.num_programs(2)

    @pl.when(pl.program_id(2) == 0)
    def _():
        st_ref[...] = jnp.zeros_like(st_ref)

    cidx = _rwkv_chunk_index(d, pl.program_id(2), nctx, nc)
    is_ctx = cidx < nctx
    first = jnp.logical_or(cidx == 0, cidx == nctx)
    last = jnp.logical_or(cidx == nctx - 1, cidx == nc - 1)
    trow = lax.broadcasted_iota(jnp.int32, (C, 1), 0)
    row = lax.broadcasted_iota(jnp.int32, (C, C), 0)
    col = lax.broadcasted_iota(jnp.int32, (C, C), 1)
    rel = (row - col) * (1 - 2 * d)
    strict = (rel > 0)[None]
    incl = (rel >= 0)[None]
    eye = (row == col).astype(F32)[None]
    tri = jnp.where(rel >= 0, 1.0, 0.0).astype(BF16)
    mm2 = (((1,), (0,)), ((), ()))

    for bi in range(nseq):
        z = jnp.where(is_ctx, zc_ref[bi], zl_ref[bi])
        prev_row = jnp.where(is_ctx, zcp_ref[bi, SUBLANE - 1:SUBLANE, :], zlp_ref[bi, SUBLANE - 1:SUBLANE, :])
        next_row = jnp.where(is_ctx, zcn_ref[bi, 0:1, :], zln_ref[bi, 0:1, :])
        prev_row = jnp.where(first, 0.0, prev_row)
        next_row = jnp.where(last, 0.0, next_row)
        zp = jnp.where(trow == 0, prev_row, pltpu.roll(z, 1, axis=0))
        zn = jnp.where(trow == C - 1, next_row, pltpu.roll(z, C - 1, axis=0))
        u_ref[bi] = z + mu_ref[0:1, :] * (zp - z) + mu_ref[1:2, :] * (zn - z)
        gs_ref[0, bi] = u_ref[bi, :, g_off:g_off + 2 * LANE]

        lora_w = u_ref[bi, :, 3 * rw:3 * rw + 2 * R_LORA_W]
        lora_a = u_ref[bi, :, 3 * rw + 2 * R_LORA_W:3 * rw + 2 * R_LORA_W + 2 * R_LORA_A]
        w_log = w0_ref[0] + _dot3(jnp.tanh(lora_w), wup_ref[0], mm2)
        nx = -w_log
        softplus = jnp.maximum(nx, 0.0) + jnp.log(1.0 + jnp.exp(-jnp.abs(nx)))
        lw = -jnp.exp(-softplus - 0.5)
        a = jax.nn.sigmoid(a0_ref[0] + _dot3(lora_a, aup_ref[0], mm2))

        l1 = lw.astype(BF16)
        r1 = lw - l1.astype(F32)
        l2 = r1.astype(BF16)
        l3 = (r1 - l2.astype(F32)).astype(BF16)
        cum = _dg(tri, l1, mm2) + (_dg(tri, l2, mm2) + _dg(tri, l3, mm2))
        g_end = jnp.exp(jnp.sum(lw, axis=0, keepdims=True))
        k = u_ref[bi, :, rw:2 * rw]
        gp_s[bi] = jnp.exp(cum - lw)
        gi_s[bi] = jnp.exp(-cum)
        a_s[bi] = a
        kn_s[bi] = k * kk_ref[...]
        kd_s[bi] = k * (1.0 + (a - 1.0) * ka_ref[...])
        rt_s[bi] = u_ref[bi, :, :rw] * jnp.exp(cum)

        for h in range(nh):
            sl = slice(h * N, (h + 1) * N)
            hh = bi * nh + h
            kk = kn_s[bi, :, sl]
            kk = kk * lax.rsqrt(jnp.maximum(jnp.sum(kk * kk, axis=-1, keepdims=True), 1e-24))
            r_h = u_ref[bi, :, h * N:(h + 1) * N]
            v_h = u_ref[bi, :, 2 * rw + h * N:2 * rw + (h + 1) * N]
            kd_h = kd_s[bi, :, sl]
            at3[hh] = -kk * gp_s[bi, :, sl]
            bt3[hh] = kk * a_s[bi, :, sl] * gi_s[bi, :, sl]
            kt3[hh] = kd_h * gi_s[bi, :, sl]
            rt3[hh] = rt_s[bi, :, sl]
            v3[hh] = v_h
            gc3[hh] = g_end[:, sl]
            bon_ref[0, bi, :, sl] = jnp.sum(r_h * kd_h * rk_ref[:, sl], axis=-1, keepdims=True) * v_h

    at = at3[...]
    bt = bt3[...]
    kt = kt3[...]
    rt = rt3[...]
    v = v3[...]
    gc = gc3[...]
    m0 = st_ref[...]
    bmm = (((2,), (1,)), ((0,), (0,)))
    bmt = (((2,), (2,)), ((0,), (0,)))
    btm = (((1,), (1,)), ((0,), (0,)))

    ar = jnp.concatenate([at, rt], axis=1)
    bk = jnp.concatenate([bt, kt], axis=1)
    big = _dot1(ar, bk, bmt)
    a_ab = jnp.where(strict, big[:, :C, :C], 0.0)
    a_ak = jnp.where(strict, big[:, :C, C:], 0.0)
    a_rb = jnp.where(incl, big[:, C:, :C], 0.0)
    a_rk = jnp.where(incl, big[:, C:, C:], 0.0)

    tinv = eye + a_ab
    p = _dot3(a_ab, a_ab, bmm)
    for _ in range(int(math.log2(C)) - 2):
        both = _dot3(jnp.concatenate([tinv, p], axis=1), p, bmm)
        tinv = tinv + both[:, :C]
        p = both[:, C:]
    tinv = tinv + _dot3(tinv, p, bmm)

    av = _dot1(jnp.concatenate([a_ak, a_rk], axis=1), v, bmm)
    akv = av[:, :C]
    w12 = _dot3(tinv, jnp.concatenate([at, akv], axis=2), bmm)
    rb12 = _dot1(a_rb, w12, bmm)
    rq = rt + rb12[:, :, :at.shape[2]]
    yc = rb12[:, :, at.shape[2]:] + av[:, C:]
    w1 = w12[:, :, :at.shape[2]]
    w2 = w12[:, :, at.shape[2]:]

    um = _dot1(jnp.concatenate([w1, rq], axis=1), m0, bmt)
    u = um[:, :C] + w2
    y3 = um[:, C:] + yc
    for bi in range(nseq):
        for h in range(nh):
            y_ref[0, bi, :, h * N:(h + 1) * N] = y3[bi * nh + h]
    upd = _dot1(jnp.concatenate([u, v], axis=1), bk, btm)
    st_ref[...] = (m0 + upd) * gc


def rwkv_scan(zc, zl, cols, mu, w0, w_up, a0, a_up, k_k, k_a, r_k):
    B, Lc, _ = zc.shape
    Ll = zl.shape[1]
    T = Lc + Ll
    rw = w0.shape[-1]
    C = RWKV_CHUNK
    N = R_HEAD
    H = rw // N
    nc = T // C
    nctx = Lc // C
    sub = C // SUBLANE
    g_off = 3 * rw + 2 * R_LORA_W + 2 * R_LORA_A

    def chunk(d, c):
        return _rwkv_chunk_index(d, c, nctx, nc)

    def seg(d, c, ctx):
        ci = chunk(d, c)
        return jnp.clip(ci, 0, nctx - 1) if ctx else jnp.clip(ci - nctx, 0, nc - nctx - 1)

    nseq = _pick(B, (RWKV_SEQS, 1))

    def main(ctx):
        return pl.BlockSpec((nseq, C, cols), lambda d, b, c: (b, seg(d, c, ctx), 0))

    def halo(ctx, after):
        nrow = (Lc if ctx else Ll) // SUBLANE

        def index(d, b, c):
            first = seg(d, c, ctx) * sub
            return (b, jnp.clip(first + sub if after else first - 1, 0, nrow - 1), 0)
        return pl.BlockSpec((nseq, SUBLANE, cols), index)

    def widen(w):
        z = jnp.zeros_like(w[0])
        return jnp.stack([jnp.concatenate([w[0], z], axis=0), jnp.concatenate([z, w[1]], axis=0)])

    dir_vec = pl.BlockSpec((1, 1, rw), lambda d, b, c: (d, 0, 0))
    dir_mat = pl.BlockSpec((1, 2 * R_LORA_W, rw), lambda d, b, c: (d, 0, 0))
    vec = pl.BlockSpec((1, rw), lambda d, b, c: (0, 0))
    out_spec = pl.BlockSpec((1, nseq, C, rw), lambda d, b, c: (d, b, chunk(d, c), 0))
    gs_spec = pl.BlockSpec((1, nseq, C, 2 * LANE), lambda d, b, c: (d, b, chunk(d, c), 0))
    full = pltpu.VMEM((nseq, C, rw), F32)
    per_head = pltpu.VMEM((nseq * H, C, N), F32)
    mu_pad = jnp.pad(mu, ((0, 0), (0, cols - mu.shape[1])))
    return pl.pallas_call(
        functools.partial(_rwkv_body, rw=rw, nctx=nctx, g_off=g_off),
        grid=(2, B // nseq, nc),
        in_specs=[main(True), halo(True, False), halo(True, True),
                  main(False), halo(False, False), halo(False, True),
                  pl.BlockSpec((2, cols), lambda d, b, c: (0, 0)),
                  dir_vec, dir_mat, dir_vec, dir_mat, vec, vec, vec],
        out_specs=[out_spec, out_spec, gs_spec],
        out_shape=[jax.ShapeDtypeStruct((2, B, T, rw), F32), jax.ShapeDtypeStruct((2, B, T, rw), F32),
                   jax.ShapeDtypeStruct((2, B, T, 2 * LANE), F32)],
        scratch_shapes=[pltpu.VMEM((nseq * H, N, N), F32), full, full, full, full, full, full,
                        per_head, per_head, per_head, per_head, per_head, pltpu.VMEM((nseq * H, 1, N), F32),
                        pltpu.VMEM((nseq, C, cols), F32)],
        compiler_params=pltpu.CompilerParams(
            dimension_semantics=("parallel", "parallel", "arbitrary"), vmem_limit_bytes=VMEM_LIMIT),
    )(zc, zc, zc, zl, zl, zl, mu_pad, w0[:, None, :], widen(w_up), a0[:, None, :], widen(a_up),
      k_k.reshape(1, rw), k_a.reshape(1, rw), r_k.reshape(1, rw))


def _rwkv_readout_body(y_ref, bon_ref, u_ref, gup_ref, lw_ref, lb_ref, o_ref, *, rw):
    N = R_HEAD
    ys = y_ref[0, 0] + y_ref[1, 0]
    gate = _dot3(jax.nn.sigmoid(u_ref[0, 0]), gup_ref[...], (((1,), (0,)), ((), ())))
    extra = bon_ref[0, 0] + bon_ref[1, 0] + lb_ref[...]
    for h in range(rw // N):
        sl = slice(h * N, (h + 1) * N)
        t = ys[:, sl]
        mean = jnp.mean(t, axis=-1, keepdims=True)
        var = jnp.mean(jnp.square(t - mean), axis=-1, keepdims=True)
        yn = (t - mean) * lax.rsqrt(var + R_GN_EPS) * lw_ref[:, sl]
        o_ref[0, :, sl] = (yn + extra[:, sl]) * gate[:, sl]


def rwkv_readout(y, bonus, gs, g_up, lnx_w, lnx_b):
    _, B, T, rw = y.shape
    tm = _pick(T, (256, 128, 64))
    gw = gs.shape[-1]
    g_pad = jnp.pad(g_up, ((0, gw - g_up.shape[0]), (0, 0)))
    pair = pl.BlockSpec((2, 1, tm, rw), lambda b, i: (0, b, i, 0))
    vec = pl.BlockSpec((1, rw), lambda b, i: (0, 0))
    return pl.pallas_call(
        functools.partial(_rwkv_readout_body, rw=rw),
        grid=(B, T // tm),
        in_specs=[pair, pair, pl.BlockSpec((1, 1, tm, gw), lambda b, i: (0, b, i, 0)),
                  pl.BlockSpec((gw, rw), lambda b, i: (0, 0)), vec, vec],
        out_specs=pl.BlockSpec((1, tm, rw), lambda b, i: (b, i, 0)),
        out_shape=jax.ShapeDtypeStruct((B, T, rw), F32),
        compiler_params=pltpu.CompilerParams(
            dimension_semantics=("parallel", "parallel"), vmem_limit_bytes=VMEM_LIMIT),
    )(y, bonus, gs, g_pad, lnx_w.reshape(1, rw), lnx_b.reshape(1, rw))


def _moe_up_body(be_ref, nu_ref, src_ref, nxt_ref, h_hbm, wg_ref, wu_ref, o_ref,
                 wgs_ref, wus_ref, rows_ref, xs_ref, sem):
    i = pl.program_id(0)
    nblk = pl.num_programs(0)
    slot = lax.rem(i, 2)
    prev = be_ref[jnp.maximum(i - 1, 0)]
    fresh = jnp.logical_or(i == 0, be_ref[i] != prev)

    def row_copy(idx_ref, r, dst_slot):
        return pltpu.make_async_copy(h_hbm.at[pl.ds(idx_ref[0, 0, r], 1), :],
                                     rows_ref.at[dst_slot, pl.ds(r, 1), :], sem.at[dst_slot])

    def request(idx_ref, dst_slot):
        for r in range(MOE_BLOCK):
            row_copy(idx_ref, r, dst_slot).start()

    def drain(dst_slot):
        for r in range(MOE_BLOCK):
            pltpu.make_async_copy(h_hbm.at[pl.ds(0, 1), :], rows_ref.at[dst_slot, pl.ds(r, 1), :],
                                  sem.at[dst_slot]).wait()

    @pl.when(i == 0)
    def _():
        request(src_ref, 0)

    @pl.when(fresh)
    def _():
        wgs_ref[...] = wg_ref[0, 0].astype(BF16)
        wus_ref[...] = wu_ref[0, 0].astype(BF16)

    drain(slot)
    xs_ref[...] = rows_ref[slot].astype(BF16)

    @pl.when(i < nu_ref[0])
    def _():
        request(nxt_ref, 1 - slot)
        x = xs_ref[...]
        g = jnp.dot(x, wgs_ref[...], preferred_element_type=F32)
        u = jnp.dot(x, wus_ref[...], preferred_element_type=F32)
        o_ref[...] = (g * jax.nn.sigmoid(g) * u).astype(o_ref.dtype)

    @pl.when(i >= nu_ref[0])
    def _():
        request(nxt_ref, 1 - slot)
        o_ref[...] = jnp.zeros_like(o_ref)

    @pl.when(i == nblk - 1)
    def _():
        drain(1 - slot)


def _moe_down_body(be_ref, nu_ref, h_ref, wd_ref, o_ref, wds_ref):
    i = pl.program_id(0)
    prev = be_ref[jnp.maximum(i - 1, 0)]
    fresh = jnp.logical_or(i == 0, be_ref[i] != prev)

    @pl.when(fresh)
    def _():
        wds_ref[...] = wd_ref[0, 0].astype(BF16)

    @pl.when(i < nu_ref[0])
    def _():
        o_ref[...] = jnp.dot(h_ref[...], wds_ref[...], preferred_element_type=F32).astype(o_ref.dtype)

    @pl.when(i >= nu_ref[0])
    def _():
        o_ref[...] = jnp.zeros_like(o_ref)


def moe_experts(h, src, block_expert, n_used, w_gate, w_up, w_down, layer):
    D = h.shape[1]
    R = src.shape[0]
    Hd = w_gate.shape[-1]
    nblk = R // MOE_BLOCK
    idx = src.reshape(nblk, 1, MOE_BLOCK)
    hid = pl.pallas_call(
        _moe_up_body,
        grid_spec=pltpu.PrefetchScalarGridSpec(
            num_scalar_prefetch=2,
            grid=(nblk,),
            in_specs=[pl.BlockSpec((1, 1, MOE_BLOCK), lambda i, be, nu: (i, 0, 0), memory_space=pltpu.SMEM),
                      pl.BlockSpec((1, 1, MOE_BLOCK), lambda i, be, nu: (jnp.minimum(i + 1, nblk - 1), 0, 0),
                                   memory_space=pltpu.SMEM),
                      pl.BlockSpec(memory_space=pl.ANY),
                      pl.BlockSpec((1, 1, D, Hd), lambda i, be, nu: (layer, be[i], 0, 0)),
                      pl.BlockSpec((1, 1, D, Hd), lambda i, be, nu: (layer, be[i], 0, 0))],
            out_specs=pl.BlockSpec((MOE_BLOCK, Hd), lambda i, be, nu: (i, 0)),
            scratch_shapes=[pltpu.VMEM((D, Hd), BF16), pltpu.VMEM((D, Hd), BF16),
                            pltpu.VMEM((2, MOE_BLOCK, D), F32), pltpu.VMEM((MOE_BLOCK, D), BF16),
                            pltpu.SemaphoreType.DMA((2,))]),
        out_shape=jax.ShapeDtypeStruct((R, Hd), BF16),
        compiler_params=pltpu.CompilerParams(
            dimension_semantics=("arbitrary",), vmem_limit_bytes=VMEM_LIMIT),
    )(block_expert, n_used, idx, idx, h, w_gate, w_up)
    return pl.pallas_call(
        _moe_down_body,
        grid_spec=pltpu.PrefetchScalarGridSpec(
            num_scalar_prefetch=2,
            grid=(nblk,),
            in_specs=[pl.BlockSpec((MOE_BLOCK, Hd), lambda i, be, nu: (i, 0)),
                      pl.BlockSpec((1, 1, Hd, D), lambda i, be, nu: (layer, be[i], 0, 0))],
            out_specs=pl.BlockSpec((MOE_BLOCK, D), lambda i, be, nu: (i, 0)),
            scratch_shapes=[pltpu.VMEM((Hd, D), BF16)]),
        out_shape=jax.ShapeDtypeStruct((R, D), F32),
        compiler_params=pltpu.CompilerParams(
            dimension_semantics=("arbitrary",), vmem_limit_bytes=VMEM_LIMIT),
    )(block_expert, n_used, hid, w_down)


def moe_layer(h, logits, w_gate, w_up, w_down, layer):
    T, D = h.shape
    lg = logits[:, :N_GROUPS]
    g_idx = jnp.argmax(lg, axis=-1)
    p_group = jnp.take_along_axis(jax.nn.softmax(lg, axis=-1), g_idx[:, None], axis=-1)
    le = logits[:, N_GROUPS:N_GROUPS + N_EXPERTS].reshape(T, N_GROUPS, EXPERTS_PER_GROUP)
    le_sel = jnp.take_along_axis(le, g_idx[:, None, None], axis=1)[:, 0]
    top_v, top_i = lax.top_k(le_sel, TOP_K)
    gate = p_group * jax.nn.softmax(top_v, axis=-1)
    expert = (g_idx[:, None] * EXPERTS_PER_GROUP + top_i).astype(jnp.int32)

    A = T * TOP_K
    e_flat = expert.reshape(A)
    onehot = (e_flat[:, None] == jnp.arange(N_EXPERTS, dtype=jnp.int32)[None, :]).astype(jnp.int32)
    rank = jnp.take_along_axis(jnp.cumsum(onehot, axis=0), e_flat[:, None], axis=1)[:, 0] - 1
    counts = jnp.sum(onehot, axis=0)
    padded = (counts + MOE_BLOCK - 1) // MOE_BLOCK * MOE_BLOCK
    pad_end = jnp.cumsum(padded)
    pad_start = pad_end - padded
    dest = pad_start[e_flat] + rank
    nblk = -(-A // MOE_BLOCK) + N_EXPERTS
    R = nblk * MOE_BLOCK
    n_used = (pad_end[-1] // MOE_BLOCK).astype(jnp.int32)
    blk = jnp.arange(nblk, dtype=jnp.int32)
    be = jnp.sum((pad_end[None, :] <= (blk * MOE_BLOCK)[:, None]).astype(jnp.int32), axis=1)
    be = jnp.minimum(be, N_EXPERTS - 1)
    be = jnp.where(blk < n_used, be, be[jnp.maximum(n_used - 1, 0)])
    src = jnp.zeros((R,), jnp.int32).at[dest].set(jnp.arange(A, dtype=jnp.int32) // TOP_K)
    ys = moe_experts(h, src, be, n_used.reshape(1), w_gate, w_up, w_down, layer)
    d2 = dest.reshape(T, TOP_K)
    return jnp.take(ys, d2[:, 0], axis=0, mode='clip'), jnp.take(ys, d2[:, 1], axis=0, mode='clip'), gate


def _moe_combine_body(r_ref, y0_ref, y1_ref, g_ref, gt_ref, fw_ref, o_ref, *, final):
    g = g_ref[...]
    f = g[:, 0:1] * y0_ref[...] + g[:, 1:2] * y1_ref[...]
    out = r_ref[...] + gt_ref[0] * f
    if final:
        out = out * lax.rsqrt(jnp.mean(out * out, axis=-1, keepdims=True) + NORM_EPS) * fw_ref[...]
    o_ref[...] = out


def moe_combine(r, y0, y1, g, gate, rows_per_group, row_off, final_gain=None):
    M, D = r.shape
    tm = _pick(math.gcd(math.gcd(M, rows_per_group), row_off if row_off else M), (256, 128, 64, 32, 16, 8))
    bpg = rows_per_group // tm
    off = row_off // tm
    fw = jnp.ones((1, D), F32) if final_gain is None else final_gain.reshape(1, D).astype(F32)
    return pl.pallas_call(
        functools.partial(_moe_combine_body, final=final_gain is not None),
        grid=(M // tm,),
        in_specs=[pl.BlockSpec((tm, D), lambda i: (i, 0)),
                  pl.BlockSpec((tm, D), lambda i: (i + off, 0)),
                  pl.BlockSpec((tm, D), lambda i: (i + off, 0)),
                  pl.BlockSpec((tm, TOP_K), lambda i: (i + off, 0)),
                  pl.BlockSpec((1, 1, D), lambda i: (i // bpg, 0, 0)),
                  pl.BlockSpec((1, D), lambda i: (0, 0))],
        out_specs=pl.BlockSpec((tm, D), lambda i: (i, 0)),
        out_shape=jax.ShapeDtypeStruct((M, D), F32),
        compiler_params=pltpu.CompilerParams(
            dimension_semantics=("parallel",), vmem_limit_bytes=VMEM_LIMIT),
    )(r, y0, y1, g, gate, fw)


def _rope_tables(L):
    half = A_HEAD_DIM // 2
    inv_freq = ROPE_BASE ** (-jnp.arange(0, half, 2, dtype=F32) / half)
    t = jnp.arange(L)
    row = (t // GRID_W).astype(F32)
    col = (t % GRID_W).astype(F32)
    ar = row[:, None] * inv_freq[None, :]
    ac = col[:, None] * inv_freq[None, :]
    cos = jnp.concatenate([jnp.cos(ar), jnp.cos(ar), jnp.cos(ac), jnp.cos(ac)], axis=-1)
    sin = jnp.concatenate([-jnp.sin(ar), jnp.sin(ar), -jnp.sin(ac), jnp.sin(ac)], axis=-1)
    reps = LANE // A_HEAD_DIM
    return jnp.tile(cos, (1, reps)), jnp.tile(sin, (1, reps))


def _rwkv_mixer(zc, zl, cols, mu, w0, w_up, a0, a_up, g_up, k_k, k_a, r_k, lnx_w, lnx_b):
    y, bonus, gs = rwkv_scan(zc, zl, cols, mu, w0, w_up, a0, a_up, k_k, k_a, r_k)
    return rwkv_readout(y, bonus, gs, g_up, lnx_w, lnx_b)


def _dft_table(L):
    N = 2 * L
    half = DFT_TILE // 2
    k = jnp.arange(L, dtype=jnp.int32)
    n = jnp.arange(L, dtype=jnp.int32)
    m = ((2 * k + 1)[:, None] * n[None, :]) % (2 * N)
    ang = m.astype(F32) * (math.pi / N)
    tab = jnp.stack([jnp.cos(ang).reshape(L // half, half, L), (-jnp.sin(ang)).reshape(L // half, half, L)], axis=1)
    return tab.reshape(2 * L, L)


def _hyena_filters(L, w1, b1, f1, w2, b2, f2, w3, width):
    hp = lax.Precision.HIGHEST
    t = jnp.linspace(0.0, 1.0, L, dtype=F32)[:, None]
    bands = jnp.linspace(1e-4, HY_BANDS - 1, HY_BANDS, dtype=F32)[None, :]
    ang = (2 * math.pi) * jnp.arange(L, dtype=F32)[:, None] / L * bands
    z = jnp.concatenate([t, jnp.cos(ang), -jnp.sin(ang)], axis=-1)
    h = jnp.sin(f1 * (jnp.dot(z, w1, precision=hp) + b1))
    h = jnp.sin(f2 * (jnp.dot(h, w2, precision=hp) + b2))
    h = jnp.dot(h, w3, precision=hp)
    deltas = jnp.abs(jnp.linspace(HY_MIN_DECAY, HY_MAX_DECAY, width, dtype=F32))
    h = h * jnp.exp(-t * jnp.tile(deltas, 2)[None, :])
    row0 = (jnp.arange(L) > 0).astype(F32)[:, None]
    h_bwd = h[:, width:] * row0
    h = jnp.concatenate([h[:, :width], h_bwd], axis=1)
    norm = jnp.sum(jnp.abs(h[:, :width]), axis=0, keepdims=True) + jnp.sum(jnp.abs(h_bwd), axis=0, keepdims=True)
    return h, norm


def _hy_pre_body(z_ref, zp_ref, zn_ref, w_ref, b_ref, hb_ref, u_ref, t_ref, x0_ref, *, width):
    i = pl.program_id(1)
    last = pl.num_programs(1) - 1
    z = z_ref[0]
    tl = z.shape[0]
    prev_row = jnp.where(i > 0, zp_ref[0, SUBLANE - 1:SUBLANE, :], 0.0)
    next_row = jnp.where(i < last, zn_ref[0, 0:1, :], 0.0)
    row = lax.broadcasted_iota(jnp.int32, (tl, 1), 0)
    up = jnp.where(row == 0, prev_row, pltpu.roll(z, 1, axis=0))
    dn = jnp.where(row == tl - 1, next_row, pltpu.roll(z, tl - 1, axis=0))
    c = up * w_ref[0:1, :] + z * w_ref[1:2, :] + dn * w_ref[2:3, :] + b_ref[...]
    u = c[:, 2 * width:] * c[:, width:2 * width]
    u_ref[0] = u.astype(u_ref.dtype)
    t_ref[0] = u * hb_ref[...]
    x0_ref[0] = c[:, :width]


def hyena_pre(z, conv_w, conv_b, bias):
    B, L, W3 = z.shape
    W = W3 // 3
    tl = _pick(L, (256, 128, 64, 32, 16, 8))
    nsub = tl // SUBLANE
    out = pl.BlockSpec((1, tl, W), lambda b, i: (b, i, 0))
    return pl.pallas_call(
        functools.partial(_hy_pre_body, width=W),
        grid=(B, L // tl),
        in_specs=[pl.BlockSpec((1, tl, W3), lambda b, i: (b, i, 0)),
                  pl.BlockSpec((1, SUBLANE, W3), lambda b, i: (b, jnp.maximum(i * nsub - 1, 0), 0)),
                  pl.BlockSpec((1, SUBLANE, W3), lambda b, i: (b, jnp.minimum((i + 1) * nsub, L // SUBLANE - 1), 0)),
                  pl.BlockSpec((3, W3), lambda b, i: (0, 0)),
                  pl.BlockSpec((1, W3), lambda b, i: (0, 0)),
                  pl.BlockSpec((1, W), lambda b, i: (0, 0))],
        out_specs=[out, out, out],
        out_shape=[jax.ShapeDtypeStruct((B, L, W), BF16), jax.ShapeDtypeStruct((B, L, W), F32),
                   jax.ShapeDtypeStruct((B, L, W), F32)],
        compiler_params=pltpu.CompilerParams(
            dimension_semantics=("parallel", "parallel"), vmem_limit_bytes=VMEM_LIMIT),
    )(z, z, z, conv_w, conv_b.reshape(1, W3), bias.reshape(1, W))


def _hyena_core(z, conv_w, conv_b, filt, bias):
    B, L, W3 = z.shape
    W = W3 // 3
    h, norm = filt
    u, t, x0 = hyena_pre(z, conv_w, conv_b, bias)
    tab = _dft_table(L)
    fwd = tab.astype(BF16)
    inv = tab.T.astype(BF16)
    hf = mm(fwd, h)
    half = DFT_TILE // 2
    sign = jnp.tile(jnp.concatenate([jnp.ones((half, 1), F32), -jnp.ones((half, 1), F32)]), (2 * L // DFT_TILE, 1))
    kf = (hf[:, :W] + sign * hf[:, W:]) / norm
    yf = mm(fwd, u, cmul=kf, out_dtype=BF16, nb=B)
    return mm(inv, yf, post=(t, x0, 1.0 / L), nb=B)


def kernel(x, c, ctx, c_ctx, mod_w, mod_b, norm1_w, norm2_w, router_group_w, router_expert_w, expert_w_gate,
           expert_w_up, expert_w_down, mix_w_in, mix_w_out, attn_sink, rwkv_shift_mu, rwkv_w0, rwkv_w_up,
           rwkv_a0, rwkv_a_up, rwkv_g_up, rwkv_k_k, rwkv_k_a, rwkv_r_k, rwkv_lnx_w, rwkv_lnx_b, hy_w_in,
           hy_b_in, hy_conv_w, hy_conv_b, hy_ffn_w1, hy_ffn_b1, hy_sin_f1, hy_ffn_w2, hy_ffn_b2, hy_sin_f2,
           hy_ffn_w3, hy_bias, hy_w_out, final_norm_w):
    B, L, D = x.shape
    Cn = ctx.shape[1]
    depth = mod_w.shape[0]
    a_q = D // 2
    a_kv = a_q // A_HEAD_DIM // A_GROUP * A_HEAD_DIM
    a_cols = a_q + 2 * a_kv
    rw = D // 2
    in_cols = mix_w_in.shape[-1]
    r_cols = -(-(in_cols - a_cols) // (2 * LANE)) * (2 * LANE)
    q_off = -(-r_cols // a_q) * a_q
    in_pad = q_off + a_cols
    assert in_pad % (2 * LANE) == 0

    cc = jnp.concatenate([c, c_ctx[None], jnp.zeros((SUBLANE - B - 1, D), F32)], axis=0)
    sc = jax.nn.silu(cc)
    mods = [mm(sc, mod_w, w_idx=i, bias=mod_b[i]).reshape(SUBLANE, 6, D) for i in range(depth)]
    cos, sin = _rope_tables(L)
    rope = (cos, sin, cos, sin)

    xl = x.reshape(B * L, D)
    xc = ctx.reshape(B * Cn, D)
    for i in range(depth):
        ctx_later = any(j % 2 == 0 for j in range(i + 1, depth))
        ctx_here = ctx_later or (i % 2 == 0)
        ml_ = mods[i][:B]
        mc_ = jnp.broadcast_to(mods[i][B:B + 1], (B, 6, D))

        def nrm(m, gain, a, b):
            return ((gain * (1 + m[:, b]))[:, None, :], m[:, a][:, None, :])

        n1l = nrm(ml_, norm1_w[i], 0, 1)
        n1c = nrm(mc_, norm1_w[i], 0, 1)
        gate_l = ml_[:, 2][:, None, :]
        gate_c = mc_[:, 2][:, None, :]
        if i % 2 == 0:
            e = i // 2
            wm = mix_w_in[e]
            w_in = jnp.concatenate([wm[:, a_cols:], jnp.zeros((D, q_off - (in_cols - a_cols)), F32),
                                    wm[:, :a_cols]], axis=1).astype(BF16)
            zl = mm(xl, w_in, norm=n1l, rows_per_group=L, tn=in_pad // 2).reshape(B, L, in_pad)
            zc = mm(xc, w_in, norm=n1c, rows_per_group=Cn, tn=in_pad // 2).reshape(B, Cn, in_pad)
            att_l = attention(zl, zc, attn_sink[e], rope, a_q=a_q, a_kv=a_kv, q_off=q_off, band=True)
            rw_all = _rwkv_mixer(zc, zl, r_cols, rwkv_shift_mu[e], rwkv_w0[e], rwkv_w_up[e], rwkv_a0[e],
                                 rwkv_a_up[e], rwkv_g_up[e], rwkv_k_k[e], rwkv_k_a[e], rwkv_r_k[e],
                                 rwkv_lnx_w[e], rwkv_lnx_b[e])
            w_out = mix_w_out[e].astype(BF16)
            xl = mm(att_l.reshape(B * L, a_q), w_out, x2=rw_all[:, Cn:].reshape(B * L, rw),
                    res=(xl, gate_l), rows_per_group=L)
            if ctx_later:
                att_c = attention(zl, zc, attn_sink[e], None, a_q=a_q, a_kv=a_kv, q_off=q_off, band=False)
                xc = mm(att_c.reshape(B * Cn, a_q), w_out, x2=rw_all[:, :Cn].reshape(B * Cn, rw),
                        res=(xc, gate_c), rows_per_group=Cn)
        else:
            o = i // 2
            w_in = hy_w_in[o].astype(BF16)
            w_out = hy_w_out[o].astype(BF16)
            fl = (hy_ffn_w1[o], hy_ffn_b1[o], hy_sin_f1[o], hy_ffn_w2[o], hy_ffn_b2[o], hy_sin_f2[o], hy_ffn_w3[o])
            zl = mm(xl, w_in, norm=n1l, bias=hy_b_in[o], rows_per_group=L).reshape(B, L, 3 * D)
            gl = _hyena_core(zl, hy_conv_w[o], hy_conv_b[o], _hyena_filters(L, *fl, D), hy_bias[o])
            xl = mm(gl.reshape(B * L, D), w_out, res=(xl, gate_l), rows_per_group=L)
            if ctx_later:
                zc = mm(xc, w_in, norm=n1c, bias=hy_b_in[o], rows_per_group=Cn).reshape(B, Cn, 3 * D)
                gc = _hyena_core(zc, hy_conv_w[o], hy_conv_b[o], _hyena_filters(Cn, *fl, D), hy_bias[o])
                xc = mm(gc.reshape(B * Cn, D), w_out, res=(xc, gate_c), rows_per_group=Cn)

        w_route = jnp.concatenate([router_group_w[i], router_expert_w[i]], axis=1)
        w_route = jnp.pad(w_route, ((0, 0), (0, LANE - w_route.shape[1])))
        n2l = nrm(ml_, norm2_w[i], 3, 4)
        lg_l, h_l = mm(xl, w_route, norm=n2l, rows_per_group=L, x3=True, emit_h=F32)
        if ctx_later:
            n2c = nrm(mc_, norm2_w[i], 3, 4)
            lg_c, h_c = mm(xc, w_route, norm=n2c, rows_per_group=Cn, x3=True, emit_h=F32)
            y0, y1, g = moe_layer(jnp.concatenate([h_l, h_c], axis=0), jnp.concatenate([lg_l, lg_c], axis=0),
                                  expert_w_gate, expert_w_up, expert_w_down, i)
            xl = moe_combine(xl, y0, y1, g, ml_[:, 5][:, None, :], L, 0)
            xc = moe_combine(xc, y0, y1, g, mc_[:, 5][:, None, :], Cn, B * L)
        else:
            y0, y1, g = moe_layer(h_l, lg_l, expert_w_gate, expert_w_up, expert_w_down, i)
            xl = moe_combine(xl, y0, y1, g, ml_[:, 5][:, None, :], L, 0,
                             final_gain=final_norm_w if i == depth - 1 else None)

    return xl.reshape(B, L, D)
```

```python
import functools
import math

import jax
import jax.numpy as jnp
from jax import lax
from jax.experimental import pallas as pl
from jax.experimental.pallas import tpu as pltpu

F32 = jnp.float32
BF16 = jnp.bfloat16

GRID_W = 64
NORM_EPS = 1e-6
A_HEAD_DIM = 64
A_GROUP = 8
A_WINDOW = 128
A_BLOCK = 128
ROPE_BASE = 10000.0
NEG_INF = -1e30
R_HEAD = 64
R_LORA_W = 64
R_LORA_A = 64
R_GN_EPS = 64e-5
HY_BANDS = 16
HY_MIN_DECAY = math.log(1e-2) / 1.5
HY_MAX_DECAY = math.log(1e-2) / 0.3
N_GROUPS = 4
EXPERTS_PER_GROUP = 8
N_EXPERTS = N_GROUPS * EXPERTS_PER_GROUP
TOP_K = 2

LANE = 128
SUBLANE = 8
VMEM_LIMIT = 56 * 1024 * 1024
MM_TILE_BYTES = 8 * 1024 * 1024
DFT_TILE = 512
RWKV_CHUNK = 64
RWKV_SEQS = 2
MOE_BLOCK = 256


def _pick(n, cands):
    for c in cands:
        if n % c == 0:
            return c
    return n


def _split_bf16(x):
    hi = x.astype(BF16)
    lo = (x - hi.astype(F32)).astype(BF16)
    return hi, lo


def _dg(a, b, dims):
    return lax.dot_general(a, b, dims, preferred_element_type=F32)


def _dot3(a, b, dims):
    ah, al = _split_bf16(a)
    bh, bl = _split_bf16(b)
    ax = a.ndim - 2
    m = a.shape[ax]
    if dims[0][0][0] == ax:
        return _dg(ah, bh, dims) + (_dg(ah, bl, dims) + _dg(al, bh, dims))
    both = _dg(jnp.concatenate([ah, al], axis=ax), bh, dims)
    return (lax.slice_in_dim(both, 0, m, axis=ax) + lax.slice_in_dim(both, m, 2 * m, axis=ax)) + _dg(ah, bl, dims)


def _dot1(a, b, dims):
    return _dg(a.astype(BF16), b.astype(BF16), dims)


def _mm_body(*refs, norm, bias, res, x3, emit_h, w3d, two_x, cmul, post, batched):
    it = iter(refs)
    x_ref = next(it)
    x2_ref = next(it) if two_x else None
    w_ref = next(it)
    g_ref = next(it) if norm else None
    s_ref = next(it) if norm else None
    b_ref = next(it) if bias else None
    r_ref = next(it) if res else None
    gt_ref = next(it) if res else None
    k_ref = next(it) if cmul else None
    pt_ref = next(it) if post else None
    px_ref = next(it) if post else None
    o_ref = next(it)
    h_ref = next(it) if emit_h else None
    xh_ref = next(it)
    xl_ref = next(it) if x3 else None

    @pl.when(pl.program_id(2 if batched else 1) == 0)
    def _():
        x = x_ref[...].astype(F32)
        if norm:
            ms = jnp.mean(x * x, axis=-1, keepdims=True)
            x = x * lax.rsqrt(ms + NORM_EPS) * g_ref[0] + s_ref[0]
        xh = x.astype(BF16)
        if two_x:
            xh_ref[:, :x.shape[1]] = xh
            xh_ref[:, x.shape[1]:] = x2_ref[...].astype(BF16)
        else:
            xh_ref[...] = xh
        if x3:
            xl_ref[...] = (x - xh.astype(F32)).astype(BF16)
        if emit_h:
            h_ref[...] = x.astype(h_ref.dtype)

    w = w_ref[0] if (w3d or batched) else w_ref[...]
    wh = w.astype(BF16)
    acc = jnp.dot(xh_ref[...], wh, preferred_element_type=F32)
    if x3:
        wl = (w.astype(F32) - wh.astype(F32)).astype(BF16)
        acc = acc + (jnp.dot(xh_ref[...], wl, preferred_element_type=F32)
                     + jnp.dot(xl_ref[...], wh, preferred_element_type=F32))
    if bias:
        acc = acc + b_ref[...]
    if res:
        acc = r_ref[...] + gt_ref[0] * acc
    if cmul:
        h = acc.shape[0] // 2
        re, im = acc[:h], acc[h:]
        kr, ki = k_ref[:h, :], k_ref[h:, :]
        acc = jnp.concatenate([re * kr - im * ki, re * ki + im * kr], axis=0)
    if post:
        acc = (acc * post + (pt_ref[0] if batched else pt_ref[...])) * (px_ref[0] if batched else px_ref[...])
    if batched:
        o_ref[0] = acc.astype(o_ref.dtype)
    else:
        o_ref[...] = acc.astype(o_ref.dtype)


def _mm_tiles(M, K, N, rows_per_group, w_itemsize):
    tm = _pick(math.gcd(M, rows_per_group), (512, 256, 128, 64, 32, 16, 8))
    cands = [t for t in (2048, 1536, 1024, 768, 640, 512, 384, 256, 128)
             if K * t * w_itemsize <= MM_TILE_BYTES and tm * t * 4 <= MM_TILE_BYTES]
    tn = N if (N <= cands[0] and N % LANE == 0) else _pick(N, cands)
    return tm, tn


def mm(x, w, *, x2=None, w_idx=None, norm=None, bias=None, res=None, rows_per_group=None, x3=False,
       emit_h=None, out_dtype=F32, tn=None, cmul=None, post=None, nb=None):
    M, K1 = x.shape
    K = K1 if x2 is None else K1 + x2.shape[1]
    assert x2 is None or (norm is None and not x3 and emit_h is None)
    N = w.shape[-1]
    if rows_per_group is None:
        rows_per_group = M
    tm, tn_auto = _mm_tiles(M, K, N, rows_per_group, w.dtype.itemsize)
    tn = tn_auto if tn is None else tn
    assert M % tm == 0 and N % tn == 0 and rows_per_group % tm == 0
    bpg = rows_per_group // tm
    assert nb is None or (w_idx is None and x2 is None and norm is None and bias is None and res is None
                          and emit_h is None)

    def shared(shape, f):
        return pl.BlockSpec(shape, f if nb is None else (lambda b, i, j: f(i, j)))

    def per_problem(shape, f):
        if nb is None:
            return pl.BlockSpec(shape, f)
        return pl.BlockSpec((1,) + shape, lambda b, i, j: (b,) + f(i, j))

    if w_idx is None:
        w_spec = per_problem((K, tn), lambda i, j: (0, j))
    else:
        w_spec = pl.BlockSpec((1, K, tn), lambda i, j: (w_idx, 0, j))
    in_specs = [shared((tm, K1), lambda i, j: (i, 0))]
    args = [x]
    if x2 is not None:
        in_specs.append(pl.BlockSpec((tm, K - K1), lambda i, j: (i, 0)))
        args.append(x2)
    in_specs.append(w_spec)
    args.append(w)
    if norm is not None:
        in_specs += [pl.BlockSpec((1, 1, K), lambda i, j: (i // bpg, 0, 0))] * 2
        args += [norm[0], norm[1]]
    if bias is not None:
        in_specs.append(pl.BlockSpec((1, tn), lambda i, j: (0, j)))
        args.append(bias.reshape(1, N).astype(F32))
    if res is not None:
        in_specs.append(pl.BlockSpec((tm, tn), lambda i, j: (i, j)))
        in_specs.append(pl.BlockSpec((1, 1, tn), lambda i, j: (i // bpg, 0, j)))
        args += [res[0], res[1]]
    if cmul is not None:
        in_specs.append(shared((tm, tn), lambda i, j: (i, j)))
        args.append(cmul)
    if post is not None:
        in_specs += [per_problem((tm, tn), lambda i, j: (i, j))] * 2
        args += [post[0], post[1]]
    out_shape = [jax.ShapeDtypeStruct((M, N) if nb is None else (nb, M, N), out_dtype)]
    out_specs = [per_problem((tm, tn), lambda i, j: (i, j))]
    if emit_h is not None:
        out_shape.append(jax.ShapeDtypeStruct((M, K), emit_h))
        out_specs.append(pl.BlockSpec((tm, K), lambda i, j: (i, 0)))
    scratch = [pltpu.VMEM((tm, K), BF16)]
    if x3:
        scratch.append(pltpu.VMEM((tm, K), BF16))
    body = functools.partial(_mm_body, norm=norm is not None, bias=bias is not None,
                             res=res is not None, x3=x3, emit_h=emit_h is not None,
                             w3d=w_idx is not None, two_x=x2 is not None, cmul=cmul is not None,
                             post=None if post is None else post[2], batched=nb is not None)
    grid = (M // tm, N // tn)
    sem = ("parallel", "arbitrary")
    outs = pl.pallas_call(
        body,
        grid=grid if nb is None else (nb,) + grid,
        in_specs=in_specs,
        out_specs=out_specs,
        out_shape=out_shape,
        scratch_shapes=scratch,
        compiler_params=pltpu.CompilerParams(
            dimension_semantics=sem if nb is None else ("parallel",) + sem, vmem_limit_bytes=VMEM_LIMIT),
    )(*args)
    return outs if emit_h is not None else outs[0]


def _swap16(x):
    q4 = A_HEAD_DIM // 4
    lane = lax.broadcasted_iota(jnp.int32, x.shape, 1)
    return jnp.where((lane & q4) == 0, pltpu.roll(x, LANE - q4, axis=1), pltpu.roll(x, q4, axis=1))


def _rope_tile(x, cos, sin):
    parts = []
    for j in range(x.shape[1] // LANE):
        t = x[:, j * LANE:(j + 1) * LANE]
        parts.append(t * cos + _swap16(t) * sin)
    return parts[0] if len(parts) == 1 else jnp.concatenate(parts, axis=1)


def _attn_body(*refs, seq, n_kv, band):
    dh = A_HEAD_DIM
    if band:
        (sink_ref, q_ref, k_ref, v_ref, kc_ref, vc_ref, cq_ref, sq_ref, ck_ref, sk_ref,
         o_ref, qs_ref, ks_ref, vs_ref, kcs_ref, vcs_ref) = refs
    else:
        sink_ref, q_ref, kc_ref, vc_ref, o_ref, qs_ref, kcs_ref, vcs_ref = refs
    n = pl.program_id(1)
    scale = dh ** -0.5
    kcs_ref[...] = kc_ref[0].astype(BF16)
    vcs_ref[...] = vc_ref[0].astype(BF16)
    if band:
        win = 3 * A_BLOCK
        start = jnp.clip((n - 1) * A_BLOCK, 0, seq - win)
        start = pl.multiple_of(start, A_BLOCK)
        qpos = n * A_BLOCK + lax.broadcasted_iota(jnp.int32, (A_BLOCK, win), 0)
        kpos = start + lax.broadcasted_iota(jnp.int32, (A_BLOCK, win), 1)
        mask = jnp.abs(kpos - qpos) <= A_WINDOW
        qs_ref[...] = (_rope_tile(q_ref[0], cq_ref[...], sq_ref[...]) * scale).astype(BF16)
        ks_ref[...] = _rope_tile(k_ref[0, pl.ds(start, win), :], ck_ref[pl.ds(start, win), :],
                                 sk_ref[pl.ds(start, win), :]).astype(BF16)
        vs_ref[...] = v_ref[0, pl.ds(start, win), :].astype(BF16)
    else:
        qs_ref[...] = (q_ref[0] * scale).astype(BF16)
    nt = (((1,), (1,)), ((), ()))
    for h in range(n_kv):
        hs = slice(h * dh, (h + 1) * dh)
        kc = kcs_ref[:, hs]
        vc = vcs_ref[:, hs]
        if band:
            kw = ks_ref[:, hs]
            vw = vs_ref[:, hs]
        for g in range(A_GROUP):
            hd = h * A_GROUP + g
            q = qs_ref[:, hd * dh:(hd + 1) * dh]
            sink = sink_ref[hd]
            s_c = _dg(q, kc, nt)
            m = jnp.maximum(jnp.max(s_c, axis=-1, keepdims=True), sink)
            if band:
                s_w = jnp.where(mask, _dg(q, kw, nt), NEG_INF)
                m = jnp.maximum(m, jnp.max(s_w, axis=-1, keepdims=True))
                p_w = jnp.exp(s_w - m)
            p_c = jnp.exp(s_c - m)
            den = jnp.sum(p_c, axis=-1, keepdims=True) + jnp.exp(sink - m)
            o = jnp.dot(p_c.astype(BF16), vc, preferred_element_type=F32)
            if band:
                den = den + jnp.sum(p_w, axis=-1, keepdims=True)
                o = o + jnp.dot(p_w.astype(BF16), vw, preferred_element_type=F32)
            o_ref[0, :, hd * dh:(hd + 1) * dh] = o / den


def attention(z, zc, sink, rope, *, a_q, a_kv, q_off, band):
    src = z if band else zc
    B, L, _ = src.shape
    Cn = zc.shape[1]
    n_kv = a_kv // A_HEAD_DIM
    nb = L // A_BLOCK
    qblk = q_off // a_q
    kblk = (q_off + a_q) // a_kv
    body = functools.partial(_attn_body, seq=L, n_kv=n_kv, band=band)
    q_spec = pl.BlockSpec((1, A_BLOCK, a_q), lambda b, n: (b, n, qblk))
    kc_spec = pl.BlockSpec((1, Cn, a_kv), lambda b, n: (b, 0, kblk))
    vc_spec = pl.BlockSpec((1, Cn, a_kv), lambda b, n: (b, 0, kblk + 1))
    smem = pl.BlockSpec(memory_space=pltpu.SMEM)
    if band:
        cq, sq, ck, sk = rope
        tab_q = pl.BlockSpec((A_BLOCK, LANE), lambda b, n: (n, 0))
        tab_k = pl.BlockSpec((L, LANE), lambda b, n: (0, 0))
        in_specs = [smem, q_spec,
                    pl.BlockSpec((1, L, a_kv), lambda b, n: (b, 0, kblk)),
                    pl.BlockSpec((1, L, a_kv), lambda b, n: (b, 0, kblk + 1)),
                    kc_spec, vc_spec, tab_q, tab_q, tab_k, tab_k]
        args = (sink.astype(F32), z, z, z, zc, zc, cq, sq, ck, sk)
        scratch = [pltpu.VMEM((A_BLOCK, a_q), BF16), pltpu.VMEM((3 * A_BLOCK, a_kv), BF16),
                   pltpu.VMEM((3 * A_BLOCK, a_kv), BF16), pltpu.VMEM((Cn, a_kv), BF16),
                   pltpu.VMEM((Cn, a_kv), BF16)]
    else:
        in_specs = [smem, q_spec, kc_spec, vc_spec]
        args = (sink.astype(F32), zc, zc, zc)
        scratch = [pltpu.VMEM((A_BLOCK, a_q), BF16), pltpu.VMEM((Cn, a_kv), BF16), pltpu.VMEM((Cn, a_kv), BF16)]
    return pl.pallas_call(
        body,
        grid=(B, nb),
        in_specs=in_specs,
        out_specs=pl.BlockSpec((1, A_BLOCK, a_q), lambda b, n: (b, n, 0)),
        out_shape=jax.ShapeDtypeStruct((B, L, a_q), F32),
        scratch_shapes=scratch,
        compiler_params=pltpu.CompilerParams(
            dimension_semantics=("parallel", "arbitrary"), vmem_limit_bytes=VMEM_LIMIT),
    )(*args)


def _rwkv_chunk_index(d, c, nctx, nc):
    back = jnp.where(c < nctx, nctx - 1 - c, nc - 1 - c + nctx)
    return c + d * (back - c)


def _rwkv_body(zc_ref, zcp_ref, zcn_ref, zl_ref, zlp_ref, zln_ref, mu_ref, w0_ref, wup_ref, a0_ref, aup_ref,
               kk_ref, ka_ref, rk_ref, y_ref, bon_ref, gs_ref,
               st_ref, gp_s, gi_s, a_s, kn_s, kd_s, rt_s, at3, bt3, kt3, rt3, v3, gc3, u_ref, *, rw, nctx, g_off):
    C = RWKV_CHUNK
    N = R_HEAD
    nh = rw // N
    nseq = zc_ref.shape[0]
    d = pl.program_id(0)
    nc = pl.num_programs(2)

    @pl.when(pl.program_id(2) == 0)
    def _():
        st_ref[...] = jnp.zeros_like(st_ref)

    cidx = _rwkv_chunk_index(d, pl.program_id(2), nctx, nc)
    is_ctx = cidx < nctx
    first = jnp.logical_or(cidx == 0, cidx == nctx)
    last = jnp.logical_or(cidx == nctx - 1, cidx == nc - 1)
    trow = lax.broadcasted_iota(jnp.int32, (C, 1), 0)
    row = lax.broadcasted_iota(jnp.int32, (C, C), 0)
    col = lax.broadcasted_iota(jnp.int32, (C, C), 1)
    rel = (row - col) * (1 - 2 * d)
    strict = (rel > 0)[None]
    incl = (rel >= 0)[None]
    eye = (row == col).astype(F32)[None]
    tri = jnp.where(rel >= 0, 1.0, 0.0).astype(BF16)
    mm2 = (((1,), (0,)), ((), ()))

    for bi in range(nseq):
        z = jnp.where(is_ctx, zc_ref[bi], zl_ref[bi])
        prev_row = jnp.where(is_ctx, zcp_ref[bi, SUBLANE - 1:SUBLANE, :], zlp_ref[bi, SUBLANE - 1:SUBLANE, :])
        next_row = jnp.where(is_ctx, zcn_ref[bi, 0:1, :], zln_ref[bi, 0:1, :])
        prev_row = jnp.where(first, 0.0, prev_row)
        next_row = jnp.where(last, 0.0, next_row)
        zp = jnp.where(trow == 0, prev_row, pltpu.roll(z, 1, axis=0))
        zn = jnp.where(trow == C - 1, next_row, pltpu.roll(z, C - 1, axis=0))
        u_ref[bi] = z + mu_ref[0:1, :] * (zp - z) + mu_ref[1:2, :] * (zn - z)
        gs_ref[0, bi] = u_ref[bi, :, g_off:g_off + 2 * LANE]

        lora_w = u_ref[bi, :, 3 * rw:3 * rw + 2 * R_LORA_W]
        lora_a = u_ref[bi, :, 3 * rw + 2 * R_LORA_W:3 * rw + 2 * R_LORA_W + 2 * R_LORA_A]
        w_log = w0_ref[0] + _dot3(jnp.tanh(lora_w), wup_ref[0], mm2)
        nx = -w_log
        softplus = jnp.maximum(nx, 0.0) + jnp.log(1.0 + jnp.exp(-jnp.abs(nx)))
        lw = -jnp.exp(-softplus - 0.5)
        a = jax.nn.sigmoid(a0_ref[0] + _dot3(lora_a, aup_ref[0], mm2))

        l1 = lw.astype(BF16)
        r1 = lw - l1.astype(F32)
        l2 = r1.astype(BF16)
        l3 = (r1 - l2.astype(F32)).astype(BF16)
        cum = _dg(tri, l1, mm2) + (_dg(tri, l2, mm2) + _dg(tri, l3, mm2))
        g_end = jnp.exp(jnp.sum(lw, axis=0, keepdims=True))
        k = u_ref[bi, :, rw:2 * rw]
        gp_s[bi] = jnp.exp(cum - lw)
        gi_s[bi] = jnp.exp(-cum)
        a_s[bi] = a
        kn_s[bi] = k * kk_ref[...]
        kd_s[bi] = k * (1.0 + (a - 1.0) * ka_ref[...])
        rt_s[bi] = u_ref[bi, :, :rw] * jnp.exp(cum)

        for h in range(nh):
            sl = slice(h * N, (h + 1) * N)
            hh = bi * nh + h
            kk = kn_s[bi, :, sl]
            kk = kk * lax.rsqrt(jnp.maximum(jnp.sum(kk * kk, axis=-1, keepdims=True), 1e-24))
            r_h = u_ref[bi, :, h * N:(h + 1) * N]
            v_h = u_ref[bi, :, 2 * rw + h * N:2 * rw + (h + 1) * N]
            kd_h = kd_s[bi, :, sl]
            at3[hh] = -kk * gp_s[bi, :, sl]
            bt3[hh] = kk * a_s[bi, :, sl] * gi_s[bi, :, sl]
            kt3[hh] = kd_h * gi_s[bi, :, sl]
            rt3[hh] = rt_s[bi, :, sl]
            v3[hh] = v_h
            gc3[hh] = g_end[:, sl]
            bon_ref[0, bi, :, sl] = jnp.sum(r_h * kd_h * rk_ref[:, sl], axis=-1, keepdims=True) * v_h

    at = at3[...]
    bt = bt3[...]
    kt = kt3[...]
    rt = rt3[...]
    v = v3[...]
    gc = gc3[...]
    m0 = st_ref[...]
    bmm = (((2,), (1,)), ((0,), (0,)))
    bmt = (((2,), (2,)), ((0,), (0,)))
    btm = (((1,), (1,)), ((0,), (0,)))

    ar = jnp.concatenate([at, rt], axis=1)
    bk = jnp.concatenate([bt, kt], axis=1)
    big = _dot1(ar, bk, bmt)
    a_ab = jnp.where(strict, big[:, :C, :C], 0.0)
    a_ak = jnp.where(strict, big[:, :C, C:], 0.0)
    a_rb = jnp.where(incl, big[:, C:, :C], 0.0)
    a_rk = jnp.where(incl, big[:, C:, C:], 0.0)

    def dot3_rows(rows_hi, rows_lo, rhs_hi, rhs_lo):
        m = rows_hi.shape[1]
        both = _dg(jnp.concatenate([rows_hi, rows_lo], axis=1), rhs_hi, bmm)
        return (both[:, :m] + both[:, m:]) + _dg(rows_hi, rhs_lo, bmm)

    tinv = eye + a_ab
    nh_, nl_ = _split_bf16(a_ab)
    p = dot3_rows(nh_, nl_, nh_, nl_)
    for _ in range(int(math.log2(C)) - 2):
        th, tl = _split_bf16(tinv)
        ph, pl_ = _split_bf16(p)
        both = dot3_rows(jnp.concatenate([th, ph], axis=1), jnp.concatenate([tl, pl_], axis=1), ph, pl_)
        tinv = tinv + both[:, :C]
        p = both[:, C:]
    th, tl = _split_bf16(tinv)
    ph, pl_ = _split_bf16(p)
    tinv = tinv + dot3_rows(th, tl, ph, pl_)

    av = _dot1(jnp.concatenate([a_ak, a_rk], axis=1), v, bmm)
    akv = av[:, :C]
    w12 = _dot3(tinv, jnp.concatenate([at, akv], axis=2), bmm)
    rb12 = _dot1(a_rb, w12, bmm)
    rq = rt + rb12[:, :, :at.shape[2]]
    yc = rb12[:, :, at.shape[2]:] + av[:, C:]
    w1 = w12[:, :, :at.shape[2]]
    w2 = w12[:, :, at.shape[2]:]

    um = _dot1(jnp.concatenate([w1, rq], axis=1), m0, bmt)
    u = um[:, :C] + w2
    y3 = um[:, C:] + yc
    for bi in range(nseq):
        for h in range(nh):
            y_ref[0, bi, :, h * N:(h + 1) * N] = y3[bi * nh + h]
    upd = _dot1(jnp.concatenate([u, v], axis=1), bk, btm)
    st_ref[...] = (m0 + upd) * gc


def rwkv_scan(zc, zl, cols, mu, w0, w_up, a0, a_up, k_k, k_a, r_k):
    B, Lc, _ = zc.shape
    Ll = zl.shape[1]
    T = Lc + Ll
    rw = w0.shape[-1]
    C = RWKV_CHUNK
    N = R_HEAD
    H = rw // N
    nc = T // C
    nctx = Lc // C
    sub = C // SUBLANE
    g_off = 3 * rw + 2 * R_LORA_W + 2 * R_LORA_A

    def chunk(d, c):
        return _rwkv_chunk_index(d, c, nctx, nc)

    def seg(d, c, ctx):
        ci = chunk(d, c)
        return jnp.clip(ci, 0, nctx - 1) if ctx else jnp.clip(ci - nctx, 0, nc - nctx - 1)

    nseq = _pick(B, (RWKV_SEQS, 1))

    def main(ctx):
        return pl.BlockSpec((nseq, C, cols), lambda d, b, c: (b, seg(d, c, ctx), 0))

    def halo(ctx, after):
        nrow = (Lc if ctx else Ll) // SUBLANE

        def index(d, b, c):
            first = seg(d, c, ctx) * sub
            return (b, jnp.clip(first + sub if after else first - 1, 0, nrow - 1), 0)
        return pl.BlockSpec((nseq, SUBLANE, cols), index)

    def widen(w):
        z = jnp.zeros_like(w[0])
        return jnp.stack([jnp.concatenate([w[0], z], axis=0), jnp.concatenate([z, w[1]], axis=0)])

    dir_vec = pl.BlockSpec((1, 1, rw), lambda d, b, c: (d, 0, 0))
    dir_mat = pl.BlockSpec((1, 2 * R_LORA_W, rw), lambda d, b, c: (d, 0, 0))
    vec = pl.BlockSpec((1, rw), lambda d, b, c: (0, 0))
    out_spec = pl.BlockSpec((1, nseq, C, rw), lambda d, b, c: (d, b, chunk(d, c), 0))
    gs_spec = pl.BlockSpec((1, nseq, C, 2 * LANE), lambda d, b, c: (d, b, chunk(d, c), 0))
    full = pltpu.VMEM((nseq, C, rw), F32)
    per_head = pltpu.VMEM((nseq * H, C, N), F32)
    mu_pad = jnp.pad(mu, ((0, 0), (0, cols - mu.shape[1])))
    return pl.pallas_call(
        functools.partial(_rwkv_body, rw=rw, nctx=nctx, g_off=g_off),
        grid=(2, B // nseq, nc),
        in_specs=[main(True), halo(True, False), halo(True, True),
                  main(False), halo(False, False), halo(False, True),
                  pl.BlockSpec((2, cols), lambda d, b, c: (0, 0)),
                  dir_vec, dir_mat, dir_vec, dir_mat, vec, vec, vec],
        out_specs=[out_spec, out_spec, gs_spec],
        out_shape=[jax.ShapeDtypeStruct((2, B, T, rw), F32), jax.ShapeDtypeStruct((2, B, T, rw), F32),
                   jax.ShapeDtypeStruct((2, B, T, 2 * LANE), F32)],
        scratch_shapes=[pltpu.VMEM((nseq * H, N, N), F32), full, full, full, full, full, full,
                        per_head, per_head, per_head, per_head, per_head, pltpu.VMEM((nseq * H, 1, N), F32),
                        pltpu.VMEM((nseq, C, cols), F32)],
        compiler_params=pltpu.CompilerParams(
            dimension_semantics=("parallel", "parallel", "arbitrary"), vmem_limit_bytes=VMEM_LIMIT),
    )(zc, zc, zc, zl, zl, zl, mu_pad, w0[:, None, :], widen(w_up), a0[:, None, :], widen(a_up),
      k_k.reshape(1, rw), k_a.reshape(1, rw), r_k.reshape(1, rw))


def _rwkv_readout_body(y_ref, bon_ref, u_ref, gup_ref, lw_ref, lb_ref, o_ref, *, rw):
    N = R_HEAD
    ys = y_ref[0, 0] + y_ref[1, 0]
    gate = _dot3(jax.nn.sigmoid(u_ref[0, 0]), gup_ref[...], (((1,), (0,)), ((), ())))
    extra = bon_ref[0, 0] + bon_ref[1, 0] + lb_ref[...]
    for h in range(rw // N):
        sl = slice(h * N, (h + 1) * N)
        t = ys[:, sl]
        mean = jnp.mean(t, axis=-1, keepdims=True)
        var = jnp.mean(jnp.square(t - mean), axis=-1, keepdims=True)
        yn = (t - mean) * lax.rsqrt(var + R_GN_EPS) * lw_ref[:, sl]
        o_ref[0, :, sl] = (yn + extra[:, sl]) * gate[:, sl]


def rwkv_readout(y, bonus, gs, g_up, lnx_w, lnx_b):
    _, B, T, rw = y.shape
    tm = _pick(T, (256, 128, 64))
    gw = gs.shape[-1]
    g_pad = jnp.pad(g_up, ((0, gw - g_up.shape[0]), (0, 0)))
    pair = pl.BlockSpec((2, 1, tm, rw), lambda b, i: (0, b, i, 0))
    vec = pl.BlockSpec((1, rw), lambda b, i: (0, 0))
    return pl.pallas_call(
        functools.partial(_rwkv_readout_body, rw=rw),
        grid=(B, T // tm),
        in_specs=[pair, pair, pl.BlockSpec((1, 1, tm, gw), lambda b, i: (0, b, i, 0)),
                  pl.BlockSpec((gw, rw), lambda b, i: (0, 0)), vec, vec],
        out_specs=pl.BlockSpec((1, tm, rw), lambda b, i: (b, i, 0)),
        out_shape=jax.ShapeDtypeStruct((B, T, rw), F32),
        compiler_params=pltpu.CompilerParams(
            dimension_semantics=("parallel", "parallel"), vmem_limit_bytes=VMEM_LIMIT),
    )(y, bonus, gs, g_pad, lnx_w.reshape(1, rw), lnx_b.reshape(1, rw))


def _moe_up_body(be_ref, nu_ref, src_ref, nxt_ref, h_hbm, wg_ref, wu_ref, o_ref,
                 wgs_ref, wus_ref, rows_ref, xs_ref, sem):
    i = pl.program_id(0)
    nblk = pl.num_programs(0)
    slot = lax.rem(i, 2)
    prev = be_ref[jnp.maximum(i - 1, 0)]
    fresh = jnp.logical_or(i == 0, be_ref[i] != prev)

    def row_copy(idx_ref, r, dst_slot):
        return pltpu.make_async_copy(h_hbm.at[pl.ds(idx_ref[0, 0, r], 1), :],
                                     rows_ref.at[dst_slot, pl.ds(r, 1), :], sem.at[dst_slot])

    def request(idx_ref, dst_slot):
        for r in range(MOE_BLOCK):
            row_copy(idx_ref, r, dst_slot).start()

    def drain(dst_slot):
        for r in range(MOE_BLOCK):
            pltpu.make_async_copy(h_hbm.at[pl.ds(0, 1), :], rows_ref.at[dst_slot, pl.ds(r, 1), :],
                                  sem.at[dst_slot]).wait()

    @pl.when(i == 0)
    def _():
        request(src_ref, 0)

    @pl.when(fresh)
    def _():
        wgs_ref[...] = wg_ref[0, 0].astype(BF16)
        wus_ref[...] = wu_ref[0, 0].astype(BF16)

    drain(slot)
    xs_ref[...] = rows_ref[slot].astype(BF16)

    @pl.when(i < nu_ref[0])
    def _():
        request(nxt_ref, 1 - slot)
        x = xs_ref[...]
        g = jnp.dot(x, wgs_ref[...], preferred_element_type=F32)
        u = jnp.dot(x, wus_ref[...], preferred_element_type=F32)
        o_ref[...] = (g * jax.nn.sigmoid(g) * u).astype(o_ref.dtype)

    @pl.when(i >= nu_ref[0])
    def _():
        request(nxt_ref, 1 - slot)
        o_ref[...] = jnp.zeros_like(o_ref)

    @pl.when(i == nblk - 1)
    def _():
        drain(1 - slot)


def _moe_down_body(be_ref, nu_ref, h_ref, wd_ref, o_ref, wds_ref):
    i = pl.program_id(0)
    prev = be_ref[jnp.maximum(i - 1, 0)]
    fresh = jnp.logical_or(i == 0, be_ref[i] != prev)

    @pl.when(fresh)
    def _():
        wds_ref[...] = wd_ref[0, 0].astype(BF16)

    @pl.when(i < nu_ref[0])
    def _():
        o_ref[...] = jnp.dot(h_ref[...], wds_ref[...], preferred_element_type=F32).astype(o_ref.dtype)

    @pl.when(i >= nu_ref[0])
    def _():
        o_ref[...] = jnp.zeros_like(o_ref)


def moe_experts(h, src, block_expert, n_used, w_gate, w_up, w_down, layer):
    D = h.shape[1]
    R = src.shape[0]
    Hd = w_gate.shape[-1]
    nblk = R // MOE_BLOCK
    idx = src.reshape(nblk, 1, MOE_BLOCK)
    hid = pl.pallas_call(
        _moe_up_body,
        grid_spec=pltpu.PrefetchScalarGridSpec(
            num_scalar_prefetch=2,
            grid=(nblk,),
            in_specs=[pl.BlockSpec((1, 1, MOE_BLOCK), lambda i, be, nu: (i, 0, 0), memory_space=pltpu.SMEM),
                      pl.BlockSpec((1, 1, MOE_BLOCK), lambda i, be, nu: (jnp.minimum(i + 1, nblk - 1), 0, 0),
                                   memory_space=pltpu.SMEM),
                      pl.BlockSpec(memory_space=pl.ANY),
                      pl.BlockSpec((1, 1, D, Hd), lambda i, be, nu: (layer, be[i], 0, 0)),
                      pl.BlockSpec((1, 1, D, Hd), lambda i, be, nu: (layer, be[i], 0, 0))],
            out_specs=pl.BlockSpec((MOE_BLOCK, Hd), lambda i, be, nu: (i, 0)),
            scratch_shapes=[pltpu.VMEM((D, Hd), BF16), pltpu.VMEM((D, Hd), BF16),
                            pltpu.VMEM((2, MOE_BLOCK, D), F32), pltpu.VMEM((MOE_BLOCK, D), BF16),
                            pltpu.SemaphoreType.DMA((2,))]),
        out_shape=jax.ShapeDtypeStruct((R, Hd), BF16),
        compiler_params=pltpu.CompilerParams(
            dimension_semantics=("arbitrary",), vmem_limit_bytes=VMEM_LIMIT),
    )(block_expert, n_used, idx, idx, h, w_gate, w_up)
    return pl.pallas_call(
        _moe_down_body,
        grid_spec=pltpu.PrefetchScalarGridSpec(
            num_scalar_prefetch=2,
            grid=(nblk,),
            in_specs=[pl.BlockSpec((MOE_BLOCK, Hd), lambda i, be, nu: (i, 0)),
                      pl.BlockSpec((1, 1, Hd, D), lambda i, be, nu: (layer, be[i], 0, 0))],
            out_specs=pl.BlockSpec((MOE_BLOCK, D), lambda i, be, nu: (i, 0)),
            scratch_shapes=[pltpu.VMEM((Hd, D), BF16)]),
        out_shape=jax.ShapeDtypeStruct((R, D), F32),
        compiler_params=pltpu.CompilerParams(
            dimension_semantics=("arbitrary",), vmem_limit_bytes=VMEM_LIMIT),
    )(block_expert, n_used, hid, w_down)


def moe_layer(h, logits, w_gate, w_up, w_down, layer):
    T, D = h.shape
    lg = logits[:, :N_GROUPS]
    g_idx = jnp.argmax(lg, axis=-1)
    p_group = jnp.take_along_axis(jax.nn.softmax(lg, axis=-1), g_idx[:, None], axis=-1)
    le = logits[:, N_GROUPS:N_GROUPS + N_EXPERTS].reshape(T, N_GROUPS, EXPERTS_PER_GROUP)
    le_sel = jnp.take_along_axis(le, g_idx[:, None, None], axis=1)[:, 0]
    top_v, top_i = lax.top_k(le_sel, TOP_K)
    gate = p_group * jax.nn.softmax(top_v, axis=-1)
    expert = (g_idx[:, None] * EXPERTS_PER_GROUP + top_i).astype(jnp.int32)

    A = T * TOP_K
    e_flat = expert.reshape(A)
    onehot = (e_flat[:, None] == jnp.arange(N_EXPERTS, dtype=jnp.int32)[None, :]).astype(jnp.int32)
    rank = jnp.take_along_axis(jnp.cumsum(onehot, axis=0), e_flat[:, None], axis=1)[:, 0] - 1
    counts = jnp.sum(onehot, axis=0)
    padded = (counts + MOE_BLOCK - 1) // MOE_BLOCK * MOE_BLOCK
    pad_end = jnp.cumsum(padded)
    pad_start = pad_end - padded
    dest = pad_start[e_flat] + rank
    nblk = -(-A // MOE_BLOCK) + N_EXPERTS
    R = nblk * MOE_BLOCK
    n_used = (pad_end[-1] // MOE_BLOCK).astype(jnp.int32)
    blk = jnp.arange(nblk, dtype=jnp.int32)
    be = jnp.sum((pad_end[None, :] <= (blk * MOE_BLOCK)[:, None]).astype(jnp.int32), axis=1)
    be = jnp.minimum(be, N_EXPERTS - 1)
    be = jnp.where(blk < n_used, be, be[jnp.maximum(n_used - 1, 0)])
    src = jnp.zeros((R,), jnp.int32).at[dest].set(jnp.arange(A, dtype=jnp.int32) // TOP_K)
    ys = moe_experts(h, src, be, n_used.reshape(1), w_gate, w_up, w_down, layer)
    d2 = dest.reshape(T, TOP_K)
    return jnp.take(ys, d2[:, 0], axis=0, mode='clip'), jnp.take(ys, d2[:, 1], axis=0, mode='clip'), gate


def _moe_combine_body(r_ref, y0_ref, y1_ref, g_ref, gt_ref, fw_ref, o_ref, *, final):
    g = g_ref[...]
    f = g[:, 0:1] * y0_ref[...] + g[:, 1:2] * y1_ref[...]
    out = r_ref[...] + gt_ref[0] * f
    if final:
        out = out * lax.rsqrt(jnp.mean(out * out, axis=-1, keepdims=True) + NORM_EPS) * fw_ref[...]
    o_ref[...] = out


def moe_combine(r, y0, y1, g, gate, rows_per_group, row_off, final_gain=None):
    M, D = r.shape
    tm = _pick(math.gcd(math.gcd(M, rows_per_group), row_off if row_off else M), (256, 128, 64, 32, 16, 8))
    bpg = rows_per_group // tm
    off = row_off // tm
    fw = jnp.ones((1, D), F32) if final_gain is None else final_gain.reshape(1, D).astype(F32)
    return pl.pallas_call(
        functools.partial(_moe_combine_body, final=final_gain is not None),
        grid=(M // tm,),
        in_specs=[pl.BlockSpec((tm, D), lambda i: (i, 0)),
                  pl.BlockSpec((tm, D), lambda i: (i + off, 0)),
                  pl.BlockSpec((tm, D), lambda i: (i + off, 0)),
                  pl.BlockSpec((tm, TOP_K), lambda i: (i + off, 0)),
                  pl.BlockSpec((1, 1, D), lambda i: (i // bpg, 0, 0)),
                  pl.BlockSpec((1, D), lambda i: (0, 0))],
        out_specs=pl.BlockSpec((tm, D), lambda i: (i, 0)),
        out_shape=jax.ShapeDtypeStruct((M, D), F32),
        compiler_params=pltpu.CompilerParams(
            dimension_semantics=("parallel",), vmem_limit_bytes=VMEM_LIMIT),
    )(r, y0, y1, g, gate, fw)


def _rope_tables(L):
    half = A_HEAD_DIM // 2
    inv_freq = ROPE_BASE ** (-jnp.arange(0, half, 2, dtype=F32) / half)
    t = jnp.arange(L)
    row = (t // GRID_W).astype(F32)
    col = (t % GRID_W).astype(F32)
    ar = row[:, None] * inv_freq[None, :]
    ac = col[:, None] * inv_freq[None, :]
    cos = jnp.concatenate([jnp.cos(ar), jnp.cos(ar), jnp.cos(ac), jnp.cos(ac)], axis=-1)
    sin = jnp.concatenate([-jnp.sin(ar), jnp.sin(ar), -jnp.sin(ac), jnp.sin(ac)], axis=-1)
    reps = LANE // A_HEAD_DIM
    return jnp.tile(cos, (1, reps)), jnp.tile(sin, (1, reps))


def _rwkv_mixer(zc, zl, cols, mu, w0, w_up, a0, a_up, g_up, k_k, k_a, r_k, lnx_w, lnx_b):
    y, bonus, gs = rwkv_scan(zc, zl, cols, mu, w0, w_up, a0, a_up, k_k, k_a, r_k)
    return rwkv_readout(y, bonus, gs, g_up, lnx_w, lnx_b)


def _dft_table(L):
    N = 2 * L
    half = DFT_TILE // 2
    k = jnp.arange(L, dtype=jnp.int32)
    n = jnp.arange(L, dtype=jnp.int32)
    m = ((2 * k + 1)[:, None] * n[None, :]) % (2 * N)
    ang = m.astype(F32) * (math.pi / N)
    tab = jnp.stack([jnp.cos(ang).reshape(L // half, half, L), (-jnp.sin(ang)).reshape(L // half, half, L)], axis=1)
    return tab.reshape(2 * L, L)


def _hyena_filters(L, w1, b1, f1, w2, b2, f2, w3, width):
    hp = lax.Precision.HIGHEST
    t = jnp.linspace(0.0, 1.0, L, dtype=F32)[:, None]
    bands = jnp.linspace(1e-4, HY_BANDS - 1, HY_BANDS, dtype=F32)[None, :]
    ang = (2 * math.pi) * jnp.arange(L, dtype=F32)[:, None] / L * bands
    z = jnp.concatenate([t, jnp.cos(ang), -jnp.sin(ang)], axis=-1)
    h = jnp.sin(f1 * (jnp.dot(z, w1, precision=hp) + b1))
    h = jnp.sin(f2 * (jnp.dot(h, w2, precision=hp) + b2))
    h = jnp.dot(h, w3, precision=hp)
    deltas = jnp.abs(jnp.linspace(HY_MIN_DECAY, HY_MAX_DECAY, width, dtype=F32))
    h = h * jnp.exp(-t * jnp.tile(deltas, 2)[None, :])
    row0 = (jnp.arange(L) > 0).astype(F32)[:, None]
    h_bwd = h[:, width:] * row0
    h = jnp.concatenate([h[:, :width], h_bwd], axis=1)
    norm = jnp.sum(jnp.abs(h[:, :width]), axis=0, keepdims=True) + jnp.sum(jnp.abs(h_bwd), axis=0, keepdims=True)
    return h, norm


def _hy_pre_body(z_ref, zp_ref, zn_ref, w_ref, b_ref, hb_ref, u_ref, t_ref, x0_ref, *, width):
    i = pl.program_id(1)
    last = pl.num_programs(1) - 1
    z = z_ref[0]
    tl = z.shape[0]
    prev_row = jnp.where(i > 0, zp_ref[0, SUBLANE - 1:SUBLANE, :], 0.0)
    next_row = jnp.where(i < last, zn_ref[0, 0:1, :], 0.0)
    row = lax.broadcasted_iota(jnp.int32, (tl, 1), 0)
    up = jnp.where(row == 0, prev_row, pltpu.roll(z, 1, axis=0))
    dn = jnp.where(row == tl - 1, next_row, pltpu.roll(z, tl - 1, axis=0))
    c = up * w_ref[0:1, :] + z * w_ref[1:2, :] + dn * w_ref[2:3, :] + b_ref[...]
    u = c[:, 2 * width:] * c[:, width:2 * width]
    u_ref[0] = u.astype(u_ref.dtype)
    t_ref[0] = u * hb_ref[...]
    x0_ref[0] = c[:, :width]


def hyena_pre(z, conv_w, conv_b, bias):
    B, L, W3 = z.shape
    W = W3 // 3
    tl = _pick(L, (256, 128, 64, 32, 16, 8))
    nsub = tl // SUBLANE
    out = pl.BlockSpec((1, tl, W), lambda b, i: (b, i, 0))
    return pl.pallas_call(
        functools.partial(_hy_pre_body, width=W),
        grid=(B, L // tl),
        in_specs=[pl.BlockSpec((1, tl, W3), lambda b, i: (b, i, 0)),
                  pl.BlockSpec((1, SUBLANE, W3), lambda b, i: (b, jnp.maximum(i * nsub - 1, 0), 0)),
                  pl.BlockSpec((1, SUBLANE, W3), lambda b, i: (b, jnp.minimum((i + 1) * nsub, L // SUBLANE - 1), 0)),
                  pl.BlockSpec((3, W3), lambda b, i: (0, 0)),
                  pl.BlockSpec((1, W3), lambda b, i: (0, 0)),
                  pl.BlockSpec((1, W), lambda b, i: (0, 0))],
        out_specs=[out, out, out],
        out_shape=[jax.ShapeDtypeStruct((B, L, W), BF16), jax.ShapeDtypeStruct((B, L, W), F32),
                   jax.ShapeDtypeStruct((B, L, W), F32)],
        compiler_params=pltpu.CompilerParams(
            dimension_semantics=("parallel", "parallel"), vmem_limit_bytes=VMEM_LIMIT),
    )(z, z, z, conv_w, conv_b.reshape(1, W3), bias.reshape(1, W))


def _hyena_core(z, conv_w, conv_b, filt, bias):
    B, L, W3 = z.shape
    W = W3 // 3
    h, norm = filt
    u, t, x0 = hyena_pre(z, conv_w, conv_b, bias)
    tab = _dft_table(L)
    fwd = tab.astype(BF16)
    inv = tab.T.astype(BF16)
    hf = mm(fwd, h)
    half = DFT_TILE // 2
    sign = jnp.tile(jnp.concatenate([jnp.ones((half, 1), F32), -jnp.ones((half, 1), F32)]), (2 * L // DFT_TILE, 1))
    kf = (hf[:, :W] + sign * hf[:, W:]) / norm
    yf = mm(fwd, u, cmul=kf, out_dtype=BF16, nb=B)
    return mm(inv, yf, post=(t, x0, 1.0 / L), nb=B)


def kernel(x, c, ctx, c_ctx, mod_w, mod_b, norm1_w, norm2_w, router_group_w, router_expert_w, expert_w_gate,
           expert_w_up, expert_w_down, mix_w_in, mix_w_out, attn_sink, rwkv_shift_mu, rwkv_w0, rwkv_w_up,
           rwkv_a0, rwkv_a_up, rwkv_g_up, rwkv_k_k, rwkv_k_a, rwkv_r_k, rwkv_lnx_w, rwkv_lnx_b, hy_w_in,
           hy_b_in, hy_conv_w, hy_conv_b, hy_ffn_w1, hy_ffn_b1, hy_sin_f1, hy_ffn_w2, hy_ffn_b2, hy_sin_f2,
           hy_ffn_w3, hy_bias, hy_w_out, final_norm_w):
    B, L, D = x.shape
    Cn = ctx.shape[1]
    depth = mod_w.shape[0]
    a_q = D // 2
    a_kv = a_q // A_HEAD_DIM // A_GROUP * A_HEAD_DIM
    a_cols = a_q + 2 * a_kv
    rw = D // 2
    in_cols = mix_w_in.shape[-1]
    r_cols = -(-(in_cols - a_cols) // (2 * LANE)) * (2 * LANE)
    q_off = -(-r_cols // a_q) * a_q
    in_pad = q_off + a_cols
    assert in_pad % (2 * LANE) == 0

    cc = jnp.concatenate([c, c_ctx[None], jnp.zeros((SUBLANE - B - 1, D), F32)], axis=0)
    sc = jax.nn.silu(cc)
    mods = [mm(sc, mod_w, w_idx=i, bias=mod_b[i]).reshape(SUBLANE, 6, D) for i in range(depth)]
    cos, sin = _rope_tables(L)
    rope = (cos, sin, cos, sin)

    xl = x.reshape(B * L, D)
    xc = ctx.reshape(B * Cn, D)
    for i in range(depth):
        ctx_later = any(j % 2 == 0 for j in range(i + 1, depth))
        ctx_here = ctx_later or (i % 2 == 0)
        ml_ = mods[i][:B]
        mc_ = jnp.broadcast_to(mods[i][B:B + 1], (B, 6, D))

        def nrm(m, gain, a, b):
            return ((gain * (1 + m[:, b]))[:, None, :], m[:, a][:, None, :])

        n1l = nrm(ml_, norm1_w[i], 0, 1)
        n1c = nrm(mc_, norm1_w[i], 0, 1)
        gate_l = ml_[:, 2][:, None, :]
        gate_c = mc_[:, 2][:, None, :]
        if i % 2 == 0:
            e = i // 2
            wm = mix_w_in[e]
            w_in = jnp.concatenate([wm[:, a_cols:], jnp.zeros((D, q_off - (in_cols - a_cols)), F32),
                                    wm[:, :a_cols]], axis=1).astype(BF16)
            zl = mm(xl, w_in, norm=n1l, rows_per_group=L, tn=in_pad // 2).reshape(B, L, in_pad)
            zc = mm(xc, w_in, norm=n1c, rows_per_group=Cn, tn=in_pad // 2).reshape(B, Cn, in_pad)
            att_l = attention(zl, zc, attn_sink[e], rope, a_q=a_q, a_kv=a_kv, q_off=q_off, band=True)
            rw_all = _rwkv_mixer(zc, zl, r_cols, rwkv_shift_mu[e], rwkv_w0[e], rwkv_w_up[e], rwkv_a0[e],
                                 rwkv_a_up[e], rwkv_g_up[e], rwkv_k_k[e], rwkv_k_a[e], rwkv_r_k[e],
                                 rwkv_lnx_w[e], rwkv_lnx_b[e])
            w_out = mix_w_out[e].astype(BF16)
            xl = mm(att_l.reshape(B * L, a_q), w_out, x2=rw_all[:, Cn:].reshape(B * L, rw),
                    res=(xl, gate_l), rows_per_group=L)
            if ctx_later:
                att_c = attention(zl, zc, attn_sink[e], None, a_q=a_q, a_kv=a_kv, q_off=q_off, band=False)
                xc = mm(att_c.reshape(B * Cn, a_q), w_out, x2=rw_all[:, :Cn].reshape(B * Cn, rw),
                        res=(xc, gate_c), rows_per_group=Cn)
        else:
            o = i // 2
            w_in = hy_w_in[o].astype(BF16)
            w_out = hy_w_out[o].astype(BF16)
            fl = (hy_ffn_w1[o], hy_ffn_b1[o], hy_sin_f1[o], hy_ffn_w2[o], hy_ffn_b2[o], hy_sin_f2[o], hy_ffn_w3[o])
            zl = mm(xl, w_in, norm=n1l, bias=hy_b_in[o], rows_per_group=L).reshape(B, L, 3 * D)
            gl = _hyena_core(zl, hy_conv_w[o], hy_conv_b[o], _hyena_filters(L, *fl, D), hy_bias[o])
            xl = mm(gl.reshape(B * L, D), w_out, res=(xl, gate_l), rows_per_group=L)
            if ctx_later:
                zc = mm(xc, w_in, norm=n1c, bias=hy_b_in[o], rows_per_group=Cn).reshape(B, Cn, 3 * D)
                gc = _hyena_core(zc, hy_conv_w[o], hy_conv_b[o], _hyena_filters(Cn, *fl, D), hy_bias[o])
                xc = mm(gc.reshape(B * Cn, D), w_out, res=(xc, gate_c), rows_per_group=Cn)

        w_route = jnp.concatenate([router_group_w[i], router_expert_w[i]], axis=1)
        w_route = jnp.pad(w_route, ((0, 0), (0, LANE - w_route.shape[1])))
        n2l = nrm(ml_, norm2_w[i], 3, 4)
        lg_l, h_l = mm(xl, w_route, norm=n2l, rows_per_group=L, x3=True, emit_h=F32)
        if ctx_later:
            n2c = nrm(mc_, norm2_w[i], 3, 4)
            lg_c, h_c = mm(xc, w_route, norm=n2c, rows_per_group=Cn, x3=True, emit_h=F32)
            y0, y1, g = moe_layer(jnp.concatenate([h_l, h_c], axis=0), jnp.concatenate([lg_l, lg_c], axis=0),
                                  expert_w_gate, expert_w_up, expert_w_down, i)
            xl = moe_combine(xl, y0, y1, g, ml_[:, 5][:, None, :], L, 0)
            xc = moe_combine(xc, y0, y1, g, mc_[:, 5][:, None, :], Cn, B * L)
        else:
            y0, y1, g = moe_layer(h_l, lg_l, expert_w_gate, expert_w_up, expert_w_down, i)
            xl = moe_combine(xl, y0, y1, g, ml_[:, 5][:, None, :], L, 0,
                             final_gain=final_norm_w if i == depth - 1 else None)

    return xl.reshape(B, L, D)
```
